```python
import jax, jax.numpy as jnp
from jax import lax
import numpy as np

D_MODEL = 2048
BATCH = 4
SEQ = 4096
DEPTH = 2
DEC_BATCH = 32
DEC_SEQ = 64
PAST_LEN = 4096

CHUNK = 64
HEAD_DIM = 128
D_A = D_MODEL // 2
N_BLOCKS_A = D_A // HEAD_DIM
H_B = D_MODEL // 4 // HEAD_DIM
H_C = D_MODEL // 4 // HEAD_DIM
D_B = H_B * HEAD_DIM
D_C = H_C * HEAD_DIM
D_MIX = D_A + D_B + D_C
CONV_W = 4
LRU_C = 8.0
N_PREV_CHUNKS = 8
BAND = N_PREV_CHUNKS * CHUNK
REL_CLIP = 256
N_REL = 2 * REL_CLIP + 1
D_FF = 4 * D_MODEL
D_IN = 2 * D_A + 3 * D_B + 3 * D_C + H_C
SPLITS = (D_A, 2 * D_A, 2 * D_A + D_B, 2 * D_A + 2 * D_B, 2 * D_A + 3 * D_B,
          2 * D_A + 3 * D_B + D_C, 2 * D_A + 3 * D_B + 2 * D_C, 2 * D_A + 3 * D_B + 3 * D_C)
ALPHA = (2 * DEPTH) ** 0.25
BETA = (8 * DEPTH) ** -0.25
Q_BLOCK = 128
SCALE = HEAD_DIM ** -0.5
LN_EPS = 1e-5
NEG = -1e30

kernel_name = "hybrid_streaming_encoder_step"


def _layer_norm(x, g, b):
    xf = x.astype(jnp.float32)
    mu = jnp.mean(xf, -1, keepdims=True)
    var = jnp.mean(jnp.square(xf - mu), -1, keepdims=True)
    return ((xf - mu) * lax.rsqrt(var + LN_EPS) * g + b).astype(x.dtype)


def _rms_norm(x, g):
    xf = x.astype(jnp.float32)
    return (xf * lax.rsqrt(jnp.mean(xf * xf, -1, keepdims=True) + LN_EPS) * g).astype(x.dtype)


def _project(x, w_in, b_forget):
    B, T = x.shape[:2]
    a_x, a_g, bq, bk, bv, cq, ck, cv, cf = jnp.split(jnp.einsum('btd,de->bte', x, w_in), SPLITS, axis=-1)
    heads = lambda t, n: t.reshape(B, T, n, HEAD_DIM)
    logf = jax.nn.log_sigmoid((cf + b_forget).astype(jnp.float32))
    return (a_x, a_g, heads(bq, H_B), heads(bk, H_B), heads(bv, H_B),
            heads(cq, H_C), heads(ck, H_C), heads(cv, H_C), logf)


def _lin_combine(c1, c2):
    a1, b1 = c1
    a2, b2 = c2
    return a1 * a2, a2 * b1 + b2


def _rglru(xa, conv_buf, h0, w_conv, b_conv, w_ra, b_ra, w_rx, b_rx, lam, reset_first):
    B, T, _ = xa.shape
    xe = jnp.concatenate([conv_buf.astype(xa.dtype), xa], axis=1)
    xc = b_conv + sum(w_conv[j] * xe[:, j:j + T] for j in range(CONV_W))
    xb = xc.reshape(B, T, N_BLOCKS_A, HEAD_DIM)
    r = jax.nn.sigmoid(jnp.einsum('btnd,nde->btne', xb, w_ra).reshape(B, T, D_A) + b_ra)
    i = jax.nn.sigmoid(jnp.einsum('btnd,nde->btne', xb, w_rx).reshape(B, T, D_A) + b_rx)
    log_a = -LRU_C * r.astype(jnp.float32) * jax.nn.softplus(-lam.astype(jnp.float32))
    a = jnp.exp(log_a)
    mult = jnp.sqrt(-jnp.expm1(2.0 * log_a))
    if reset_first:
        mult = mult.at[:, 0].set(1.0)
    u = mult * (i * xc).astype(jnp.float32)
    u = u.at[:, 0].add(a[:, 0] * h0.astype(jnp.float32))
    _, h = lax.associative_scan(_lin_combine, (a, u), axis=1)
    return h, xe[:, -(CONV_W - 1):], h[:, -1]


def _rel_bias(table, dist):
    return table[:, jnp.clip(dist, -REL_CLIP, REL_CLIP) + REL_CLIP].astype(jnp.float32)


def _band_prompt(q, k, v, rel_table):
    B, S = q.shape[:2]
    nc = S // CHUNK
    nk = (N_PREV_CHUNKS + 1) * CHUNK
    qc = q.reshape(B, nc, CHUNK, H_B, HEAD_DIM)
    pad = jnp.zeros((B, N_PREV_CHUNKS, CHUNK, H_B, HEAD_DIM), k.dtype)
    kp = jnp.concatenate([pad, k.reshape(B, nc, CHUNK, H_B, HEAD_DIM)], axis=1)
    vp = jnp.concatenate([pad, v.reshape(B, nc, CHUNK, H_B, HEAD_DIM)], axis=1)
    kband = jnp.concatenate([kp[:, j:j + nc] for j in range(N_PREV_CHUNKS + 1)], axis=2)
    vband = jnp.concatenate([vp[:, j:j + nc] for j in range(N_PREV_CHUNKS + 1)], axis=2)
    qi = jnp.arange(CHUNK)[:, None]
    km = jnp.arange(nk)[None, :]
    bias = _rel_bias(rel_table, BAND + qi - km)
    valid = (jnp.arange(nc)[:, None] * CHUNK - BAND + jnp.arange(nk)[None, :]) >= 0
    s = jnp.einsum('bnqhd,bnkhd->bnhqk', qc, kband).astype(jnp.float32) * SCALE + bias[None, None]
    s = jnp.where(valid[None, :, None, None, :], s, NEG)
    p = jax.nn.softmax(s, axis=-1).astype(v.dtype)
    o = jnp.einsum('bnhqk,bnkhd->bnqhd', p, vband)
    return o.reshape(B, S, D_B)


def _band_sample(q, k_new, v_new, k_cache, v_cache, rel_table):
    B, T = q.shape[:2]
    n_c = k_cache.shape[1]
    k = jnp.concatenate([k_cache.astype(k_new.dtype), k_new], axis=1)
    v = jnp.concatenate([v_cache.astype(v_new.dtype), v_new], axis=1)
    pos_k = jnp.concatenate([jnp.arange(n_c) - n_c, jnp.arange(T)])
    bias = _rel_bias(rel_table, jnp.arange(T)[:, None] - pos_k[None, :])
    s = jnp.einsum('bqhd,bkhd->bhqk', q, k).astype(jnp.float32) * SCALE + bias[None]
    p = jax.nn.softmax(s, axis=-1).astype(v.dtype)
    return jnp.einsum('bhqk,bkhd->bqhd', p, v).reshape(B, T, D_C)


def _fox_attend(q, fq, tq, k, v, fk, tk):
    s = jnp.einsum('bqhd,bkhd->bhqk', q, k).astype(jnp.float32) * SCALE
    s = s + (jnp.swapaxes(fq, 1, 2)[..., :, None] - jnp.swapaxes(fk, 1, 2)[..., None, :])
    s = jnp.where(tk[None, :] <= tq[:, None], s, NEG)
    p = jax.nn.softmax(s, axis=-1).astype(v.dtype)
    return jnp.einsum('bhqk,bkhd->bqhd', p, v)


def _fox_prompt(q, k, v, logf):
    B, S = q.shape[:2]
    F = jnp.cumsum(logf.astype(jnp.float32), axis=1)
    nb = S // Q_BLOCK
    qb = q.reshape(B, nb, Q_BLOCK, H_C, HEAD_DIM).transpose(1, 0, 2, 3, 4)
    fb = F.reshape(B, nb, Q_BLOCK, H_C).transpose(1, 0, 2, 3)
    tq = jnp.arange(S).reshape(nb, Q_BLOCK)
    tk = jnp.arange(S)
    o = lax.map(lambda blk: _fox_attend(blk[0], blk[1], blk[2], k, v, F, tk), (qb, fb, tq))
    return o.transpose(1, 0, 2, 3, 4).reshape(B, S, D_C)


def _fox_sample(q, k_new, v_new, logf_new, k_cache, v_cache, logf_cache):
    B, T = q.shape[:2]
    P = k_cache.shape[1]
    k = jnp.concatenate([k_cache.astype(k_new.dtype), k_new], axis=1)
    v = jnp.concatenate([v_cache.astype(v_new.dtype), v_new], axis=1)
    F = jnp.cumsum(jnp.concatenate([logf_cache.astype(jnp.float32), logf_new], axis=1), axis=1)
    o = _fox_attend(q, F[:, P:], P + jnp.arange(T), k, v, F, jnp.arange(P + T))
    return o.reshape(B, T, D_C)


def _finish_layer(x, y_a, y_b, y_c, g_mix, w_out, ln1_g, ln1_b, w_up, w_down, ln2_g, ln2_b):
    y = jnp.concatenate([_rms_norm(y_a, g_mix[:D_A]),
                         _rms_norm(y_b, g_mix[D_A:D_A + D_B]),
                         _rms_norm(y_c, g_mix[D_A + D_B:])], axis=-1)
    x = _layer_norm(ALPHA * x + jnp.einsum('btm,md->btd', y, w_out), ln1_g, ln1_b)
    hdn = jnp.square(jax.nn.relu(jnp.einsum('btd,df->btf', x, w_up)))
    return _layer_norm(ALPHA * x + jnp.einsum('btf,fd->btd', hdn, w_down), ln2_g, ln2_b)


def setup_inputs(seed: int = 0) -> dict:
    key = jax.random.key(seed)
    ks = jax.random.split(key, 32)
    f32 = jnp.float32
    nrm = lambda k, shape, s: jax.random.normal(k, shape, f32) * s
    band_rows = min(BAND, PAST_LEN)
    u = jax.random.uniform(ks[15], (DEPTH, D_A), f32, minval=0.9, maxval=0.999)
    a_base = u ** (1.0 / LRU_C)
    lru_lambda = jnp.log(a_base) - jnp.log1p(-a_base)
    return {
        "x_prompt": nrm(ks[0], (BATCH, SEQ, D_MODEL), 1.0),
        "x_sample": nrm(ks[1], (DEC_BATCH, DEC_SEQ, D_MODEL), 1.0),
        "state_conv": nrm(ks[2], (DEPTH, DEC_BATCH, CONV_W - 1, D_A), 1.0),
        "state_lru": nrm(ks[3], (DEPTH, DEC_BATCH, D_A), 0.5),
        "cache_band_k": nrm(ks[4], (DEPTH, DEC_BATCH, band_rows, H_B, HEAD_DIM), 1.0),
        "cache_band_v": nrm(ks[5], (DEPTH, DEC_BATCH, band_rows, H_B, HEAD_DIM), 1.0),
        "cache_fox_k": nrm(ks[6], (DEPTH, DEC_BATCH, PAST_LEN, H_C, HEAD_DIM), 1.0),
        "cache_fox_v": nrm(ks[7], (DEPTH, DEC_BATCH, PAST_LEN, H_C, HEAD_DIM), 1.0),
        "cache_fox_logf": jax.nn.log_sigmoid(nrm(ks[8], (DEPTH, DEC_BATCH, PAST_LEN, H_C), 1.0) + 4.0),
        "w_in": nrm(ks[9], (DEPTH, D_MODEL, D_IN), D_MODEL ** -0.5),
        "w_conv": nrm(ks[10], (DEPTH, CONV_W, D_A), CONV_W ** -0.5),
        "b_conv": nrm(ks[11], (DEPTH, D_A), 0.02),
        "w_rg_a": nrm(ks[12], (DEPTH, N_BLOCKS_A, HEAD_DIM, HEAD_DIM), HEAD_DIM ** -0.5),
        "b_rg_a": nrm(ks[13], (DEPTH, D_A), 0.02),
        "w_rg_x": nrm(ks[14], (DEPTH, N_BLOCKS_A, HEAD_DIM, HEAD_DIM), HEAD_DIM ** -0.5),
        "b_rg_x": nrm(ks[16], (DEPTH, D_A), 0.02),
        "lru_lambda": lru_lambda,
        "rel_bias": nrm(ks[17], (DEPTH, H_B, N_REL), 0.2),
        "b_forget": jax.random.uniform(ks[18], (DEPTH, H_C), f32, minval=2.0, maxval=6.0),
        "g_mix": 1.0 + nrm(ks[19], (DEPTH, D_MIX), 0.02),
        "w_out": nrm(ks[20], (DEPTH, D_MIX, D_MODEL), BETA * D_MIX ** -0.5),
        "ln1_g": 1.0 + nrm(ks[21], (DEPTH, D_MODEL), 0.02),
        "ln1_b": nrm(ks[22], (DEPTH, D_MODEL), 0.02),
        "w_up": nrm(ks[23], (DEPTH, D_MODEL, D_FF), D_MODEL ** -0.5),
        "w_down": nrm(ks[24], (DEPTH, D_FF, D_MODEL), BETA * D_FF ** -0.5),
        "ln2_g": 1.0 + nrm(ks[25], (DEPTH, D_MODEL), 0.02),
        "ln2_b": nrm(ks[26], (DEPTH, D_MODEL), 0.02),
    }


def reference(x_prompt, x_sample, state_conv, state_lru, cache_band_k, cache_band_v,
              cache_fox_k, cache_fox_v, cache_fox_logf, w_in, w_conv, b_conv, w_rg_a, b_rg_a,
              w_rg_x, b_rg_x, lru_lambda, rel_bias, b_forget, g_mix, w_out, ln1_g, ln1_b,
              w_up, w_down, ln2_g, ln2_b):
    xp, xs = x_prompt, x_sample
    Bp, S = xp.shape[:2]
    n_band_p = min(BAND, S)
    p_conv, p_lru, p_bk, p_bv, p_fk, p_fv, p_fl = [], [], [], [], [], [], []
    s_conv, s_lru, s_bk, s_bv, s_fk, s_fv, s_fl = [], [], [], [], [], [], []
    for l in range(DEPTH):
        lru_w = (w_conv[l], b_conv[l], w_rg_a[l], b_rg_a[l], w_rg_x[l], b_rg_x[l], lru_lambda[l])
        out_w = (g_mix[l], w_out[l], ln1_g[l], ln1_b[l], w_up[l], w_down[l], ln2_g[l], ln2_b[l])

        a_x, a_g, bq, bk, bv, cq, ck, cv, logf = _project(xp, w_in[l], b_forget[l])
        h, buf, h_last = _rglru(a_x, jnp.zeros((Bp, CONV_W - 1, D_A), xp.dtype),
                                jnp.zeros((Bp, D_A), jnp.float32), *lru_w, reset_first=True)
        y_a = h.astype(xp.dtype) * jax.nn.gelu(a_g)
        y_b = _band_prompt(bq, bk, bv, rel_bias[l])
        y_c = _fox_prompt(cq, ck, cv, logf)
        xp = _finish_layer(xp, y_a, y_b, y_c, *out_w)
        p_conv.append(buf)
        p_lru.append(h_last)
        p_bk.append(bk[:, S - n_band_p:])
        p_bv.append(bv[:, S - n_band_p:])
        p_fk.append(ck)
        p_fv.append(cv)
        p_fl.append(logf)

        a_x, a_g, bq, bk, bv, cq, ck, cv, logf = _project(xs, w_in[l], b_forget[l])
        h, buf, h_last = _rglru(a_x, state_conv[l], state_lru[l], *lru_w, reset_first=False)
        y_a = h.astype(xs.dtype) * jax.nn.gelu(a_g)
        y_b = _band_sample(bq, bk, bv, cache_band_k[l], cache_band_v[l], rel_bias[l])
        y_c = _fox_sample(cq, ck, cv, logf, cache_fox_k[l], cache_fox_v[l], cache_fox_logf[l])
        xs = _finish_layer(xs, y_a, y_b, y_c, *out_w)
        s_conv.append(buf)
        s_lru.append(h_last)
        s_bk.append(bk)
        s_bv.append(bv)
        s_fk.append(ck)
        s_fv.append(cv)
        s_fl.append(logf)

    st = lambda xs_list: jnp.stack(xs_list, axis=0)
    return (xp, xs,
            st(p_conv), st(p_lru), st(p_bk), st(p_bv), st(p_fk), st(p_fv), st(p_fl),
            st(s_conv), st(s_lru), st(s_bk), st(s_bv), st(s_fk), st(s_fv), st(s_fl))
```

```python
import functools

import numpy as np
import jax
import jax.numpy as jnp
from jax import lax
from jax.experimental import pallas as pl
from jax.experimental.pallas import tpu as pltpu

F32 = jnp.float32
BF16 = jnp.bfloat16

HEAD_DIM = 128
CHUNK = 64
N_PREV_CHUNKS = 8
BAND = N_PREV_CHUNKS * CHUNK
REL_CLIP = 256
CONV_W = 4
LRU_C = 8.0
LN_EPS = 1e-5
NEG = -1e30
SCALE = HEAD_DIM ** -0.5
N_HEADS = 4
D_ATT = N_HEADS * HEAD_DIM
LANES = 128
SUBLANES = 8
MIB = 1 << 20

PROJ_ROWS = 256
LRU_ROWS = 256
BAND_ROWS = 256
FOX_ROWS = 512
FOX_CACHE_ROWS = 1024
MIX_ROWS = 512
FFN_ROWS = 512
FFN_COLS = 1024


def _params(semantics, vmem_mib):
    return pltpu.CompilerParams(dimension_semantics=semantics, vmem_limit_bytes=vmem_mib * MIB)


def _resident(shape):
    return pl.BlockSpec(shape, lambda *_: (0,) * len(shape), pipeline_mode=pl.Buffered(1))


def _log_sigmoid(z):
    return jnp.minimum(z, 0.0) - jnp.log1p(jnp.exp(-jnp.abs(z)))


def _softplus(z):
    return jnp.maximum(z, 0.0) + jnp.log1p(jnp.exp(-jnp.abs(z)))


def _gelu_tanh(x):
    c = np.sqrt(2.0 / np.pi).astype(np.float32)
    return 0.5 * x * (1.0 + jnp.tanh(c * (x + 0.044715 * (x * x * x))))


def _layer_norm(z, g, b):
    mu = jnp.mean(z, axis=-1, keepdims=True)
    zc = z - mu
    var = jnp.mean(zc * zc, axis=-1, keepdims=True)
    return zc * lax.rsqrt(var + LN_EPS) * g + b


def _rms_norm(y, g):
    return y * lax.rsqrt(jnp.mean(y * y, axis=-1, keepdims=True) + LN_EPS) * g


def _dot_nt(a, b):
    return lax.dot_general(a, b, (((1,), (1,)), ((), ())), preferred_element_type=F32)


def _proj_kernel(x_ref, w_ref, bf_ref, *out_refs, widths):
    xb = x_ref[...].astype(BF16)
    lo = 0
    for o_ref, width in zip(out_refs[:-1], widths[:-1]):
        o_ref[...] = jnp.dot(xb, w_ref[:, lo:lo + width], preferred_element_type=F32)
        lo += width
    z = jnp.dot(xb, w_ref[:, lo:lo + widths[-1]], preferred_element_type=F32) + bf_ref[...]
    out_refs[-1][...] = _log_sigmoid(z)


def _proj(x2d, w_pad, bf_pad, widths):
    t, d = x2d.shape
    n = w_pad.shape[1]
    rows = min(PROJ_ROWS, t)
    return pl.pallas_call(
        functools.partial(_proj_kernel, widths=widths),
        grid=(t // rows,),
        in_specs=[pl.BlockSpec((rows, d), lambda i: (i, 0)),
                  _resident((d, n)),
                  _resident((1, widths[-1]))],
        out_specs=[pl.BlockSpec((rows, w), lambda i: (i, 0)) for w in widths],
        out_shape=[jax.ShapeDtypeStruct((t, w), F32) for w in widths],
        compiler_params=_params(("arbitrary",), 48),
    )(x2d, w_pad, bf_pad)


def _rglru_kernel(ax_ref, ag_ref, cs_ref, h0_ref, wc_ref, bc_ref, wra_ref, bra_ref, wrx_ref, brx_ref,
                  lam_ref, y_ref, buf_ref, hl_ref, h_sc, tail_sc, *, rows, n_blocks, reset_first):
    t = pl.program_id(1)
    d_a = ax_ref.shape[-1]

    @pl.when(t == 0)
    def _():
        h_sc[...] = h0_ref[0]
        tail_sc[...] = cs_ref[0]

    xa = ax_ref[0]
    ext = jnp.concatenate([tail_sc[...], xa], axis=0)
    xc = bc_ref[...] + wc_ref[CONV_W - 1:CONV_W, :] * xa
    for j in range(1, CONV_W):
        shifted = pltpu.roll(ext, j, 0)[SUBLANES:SUBLANES + rows]
        xc = xc + wc_ref[CONV_W - 1 - j:CONV_W - j, :] * shifted
    tail_sc[...] = xa[rows - SUBLANES:rows]

    xcb = xc.astype(BF16)

    def gate(w_ref, b_ref):
        parts = [jnp.dot(xcb[:, n * HEAD_DIM:(n + 1) * HEAD_DIM], w_ref[n], preferred_element_type=F32)
                 for n in range(n_blocks)]
        return jax.nn.sigmoid(jnp.concatenate(parts, axis=1) + b_ref[...])

    r = gate(wra_ref, bra_ref)
    i_gate = gate(wrx_ref, brx_ref)
    log_a = (-LRU_C) * r * _softplus(-lam_ref[...])
    a = jnp.exp(log_a)
    mult = jnp.sqrt(1.0 - a * a)
    row = lax.broadcasted_iota(jnp.int32, (rows, d_a), 0)
    if reset_first:
        mult = jnp.where(jnp.logical_and(row == 0, t == 0), 1.0, mult)
    u = mult * (i_gate * xc)

    shift = 1
    while shift < rows:
        a_prev = pltpu.roll(a, shift, 0)
        u_prev = pltpu.roll(u, shift, 0)
        live = row >= shift
        u = jnp.where(live, a * u_prev + u, u)
        a = jnp.where(live, a * a_prev, a)
        shift *= 2
    h = u + a * h_sc[...]
    h_last = h[rows - 1:rows]
    h_sc[...] = h_last

    y_ref[0] = h * _gelu_tanh(ag_ref[0])

    @pl.when(t == pl.num_programs(1) - 1)
    def _():
        buf_ref[0] = xa[rows - SUBLANES:rows]
        hl_ref[0] = h_last


def _rglru(a2, conv_state8, h0, w_conv, b_conv, w_ra, b_ra, w_rx, b_rx, lam, *, reset_first):
    b, t, d2 = a2.shape
    d_a = d2 // 2
    rows = min(LRU_ROWS, t)
    n_blocks = d_a // HEAD_DIM
    row_vec = _resident((1, d_a))
    gate_w = _resident((n_blocks, HEAD_DIM, HEAD_DIM))
    kern = functools.partial(_rglru_kernel, rows=rows, n_blocks=n_blocks, reset_first=reset_first)
    return pl.pallas_call(
        kern,
        grid=(b, t // rows),
        in_specs=[pl.BlockSpec((1, rows, d_a), lambda i, j: (i, j, 0)),
                  pl.BlockSpec((1, rows, d_a), lambda i, j: (i, j, 1)),
                  pl.BlockSpec((1, SUBLANES, d_a), lambda i, j: (i, 0, 0)),
                  pl.BlockSpec((1, 1, d_a), lambda i, j: (i, 0, 0)),
                  _resident((CONV_W, d_a)), row_vec, gate_w, row_vec, gate_w, row_vec, row_vec],
        out_specs=[pl.BlockSpec((1, rows, d_a), lambda i, j: (i, j, 0)),
                   pl.BlockSpec((1, SUBLANES, d_a), lambda i, j: (i, 0, 0)),
                   pl.BlockSpec((1, 1, d_a), lambda i, j: (i, 0, 0))],
        out_shape=[jax.ShapeDtypeStruct((b, t, d_a), F32),
                   jax.ShapeDtypeStruct((b, SUBLANES, d_a), F32),
                   jax.ShapeDtypeStruct((b, 1, d_a), F32)],
        scratch_shapes=[pltpu.VMEM((1, d_a), F32), pltpu.VMEM((SUBLANES, d_a), F32)],
        compiler_params=_params(("arbitrary", "arbitrary"), 48),
    )(a2, a2, conv_state8, h0, w_conv, b_conv, w_ra, b_ra, w_rx, b_rx, lam)


def _cumsum_kernel(x_ref, off_ref, o_ref):
    x = x_ref[...]
    n = x.shape[1]
    lane = lax.broadcasted_iota(jnp.int32, x.shape, 1)
    shift = 1
    while shift < n:
        prev = pltpu.roll(x, shift, 1)
        x = jnp.where(lane >= shift, x + prev, x)
        shift *= 2
    o_ref[...] = x + off_ref[...]


def _cumsum_lanes(x, off):
    r, n = x.shape
    return pl.pallas_call(
        _cumsum_kernel,
        grid=(1,),
        in_specs=[pl.BlockSpec((r, n), lambda i: (0, 0)), pl.BlockSpec((r, 1), lambda i: (0, 0))],
        out_specs=pl.BlockSpec((r, n), lambda i: (0, 0)),
        out_shape=jax.ShapeDtypeStruct((r, n), F32),
        compiler_params=_params(("arbitrary",), 32),
    )(x, off)


def _head(x, h):
    return x[:, h * HEAD_DIM:(h + 1) * HEAD_DIM]


def _softmax_pv(s, v_h):
    m = jnp.max(s, axis=-1, keepdims=True)
    p = jnp.exp(s - m)
    den = jnp.sum(p, axis=-1, keepdims=True)
    return jnp.dot(p.astype(BF16), v_h, preferred_element_type=F32) / den


def _online_update(h, s, v_h, m_sc, l_sc, acc_sc):
    m_prev = m_sc[h]
    m_new = jnp.maximum(m_prev, jnp.max(s, axis=-1, keepdims=True))
    alpha = jnp.exp(m_prev - m_new)
    p = jnp.exp(s - m_new)
    l_sc[h] = alpha * l_sc[h] + jnp.sum(p, axis=-1, keepdims=True)
    cols = slice(h * HEAD_DIM, (h + 1) * HEAD_DIM)
    acc_sc[:, cols] = alpha * acc_sc[:, cols] + jnp.dot(p.astype(BF16), v_h, preferred_element_type=F32)
    m_sc[h] = m_new


def _flash_init(m_sc, l_sc, acc_sc):
    m_sc[...] = jnp.full(m_sc.shape, NEG, F32)
    l_sc[...] = jnp.zeros(l_sc.shape, F32)
    acc_sc[...] = jnp.zeros(acc_sc.shape, F32)


def _flash_finish(o_ref, l_sc, acc_sc):
    for h in range(N_HEADS):
        cols = slice(h * HEAD_DIM, (h + 1) * HEAD_DIM)
        o_ref[0, :, cols] = acc_sc[:, cols] / l_sc[h]


def _band_prompt_kernel(q_ref, k0_ref, k1_ref, k2_ref, v0_ref, v1_ref, v2_ref, bias_ref, o_ref, *, rows):
    i = pl.program_id(1)
    q = q_ref[0]
    k = jnp.concatenate([k0_ref[0], k1_ref[0], k2_ref[0]], axis=0).astype(BF16)
    v = jnp.concatenate([v0_ref[0], v1_ref[0], v2_ref[0]], axis=0).astype(BF16)
    col = lax.broadcasted_iota(jnp.int32, (rows, 3 * rows), 1)
    in_stream = col >= (2 - i) * rows
    for h in range(N_HEADS):
        s = _dot_nt(_head(q, h).astype(BF16), _head(k, h)) * SCALE + bias_ref[h]
        s = jnp.where(in_stream, s, NEG)
        o_ref[0, :, h * HEAD_DIM:(h + 1) * HEAD_DIM] = _softmax_pv(s, _head(v, h))


def _band_prompt_bias(rel_table, rows):
    qpos = np.arange(rows)[:, None]
    kpos = np.arange(3 * rows)[None, :] - 2 * rows
    qc, kc = qpos // CHUNK, kpos // CHUNK
    visible = (kc <= qc) & (kc >= qc - N_PREV_CHUNKS)
    idx = np.clip(qpos - kpos, -REL_CLIP, REL_CLIP) + REL_CLIP
    return jnp.where(visible[None], rel_table[:, idx].astype(F32), NEG)


def _band_prompt(q, k, v, rel_table):
    b, s, d = q.shape
    rows = BAND_ROWS
    assert BAND == 2 * rows and s % rows == 0
    bias = _band_prompt_bias(rel_table, rows)
    cur = pl.BlockSpec((1, rows, d), lambda i, j: (i, j, 0))
    prev1 = pl.BlockSpec((1, rows, d), lambda i, j: (i, jnp.maximum(j - 1, 0), 0))
    prev2 = pl.BlockSpec((1, rows, d), lambda i, j: (i, jnp.maximum(j - 2, 0), 0))
    return pl.pallas_call(
        functools.partial(_band_prompt_kernel, rows=rows),
        grid=(b, s // rows),
        in_specs=[cur, prev2, prev1, cur, prev2, prev1, cur, _resident(bias.shape)],
        out_specs=cur,
        out_shape=jax.ShapeDtypeStruct((b, s, d), F32),
        compiler_params=_params(("arbitrary", "arbitrary"), 48),
    )(q, k, k, k, v, v, v, bias)


def _band_sample_kernel(q_ref, kn_ref, vn_ref, kc_ref, vc_ref, bias_ref, o_ref):
    q = q_ref[0]
    k = jnp.concatenate([kc_ref[0], kn_ref[0]], axis=0).astype(BF16)
    v = jnp.concatenate([vc_ref[0], vn_ref[0]], axis=0).astype(BF16)
    for h in range(N_HEADS):
        s = _dot_nt(_head(q, h).astype(BF16), _head(k, h)) * SCALE + bias_ref[h]
        o_ref[0, :, h * HEAD_DIM:(h + 1) * HEAD_DIM] = _softmax_pv(s, _head(v, h))


def _band_sample(q, k_new, v_new, k_cache, v_cache, rel_table):
    b, t, d = q.shape
    n_c = k_cache.shape[1]
    pos_k = np.concatenate([np.arange(n_c) - n_c, np.arange(t)])
    idx = np.clip(np.arange(t)[:, None] - pos_k[None, :], -REL_CLIP, REL_CLIP) + REL_CLIP
    bias = rel_table[:, idx].astype(F32)
    new = pl.BlockSpec((1, t, d), lambda i: (i, 0, 0))
    old = pl.BlockSpec((1, n_c, d), lambda i: (i, 0, 0))
    return pl.pallas_call(
        _band_sample_kernel,
        grid=(b,),
        in_specs=[new, new, new, old, old, _resident(bias.shape)],
        out_specs=new,
        out_shape=jax.ShapeDtypeStruct((b, t, d), F32),
        compiler_params=_params(("arbitrary",), 32),
    )(q, k_new, v_new, k_cache, v_cache, bias)


def _fox_prompt_kernel(q_ref, k_ref, v_ref, fq_ref, fk_ref, o_ref, m_sc, l_sc, acc_sc, *, rows):
    i = pl.program_id(1)
    j = pl.program_id(2)

    @pl.when(j == 0)
    def _():
        _flash_init(m_sc, l_sc, acc_sc)

    @pl.when(j <= i)
    def _():
        q = q_ref[0]
        k = k_ref[0].astype(BF16)
        v = v_ref[0].astype(BF16)
        fq = fq_ref[0]
        fk = fk_ref[0]
        tq = i * rows + lax.broadcasted_iota(jnp.int32, (rows, rows), 0)
        tk = j * rows + lax.broadcasted_iota(jnp.int32, (rows, rows), 1)
        causal = tk <= tq
        for h in range(N_HEADS):
            s = _dot_nt(_head(q, h).astype(BF16), _head(k, h)) * SCALE + (fq[:, h:h + 1] - fk[h:h + 1, :])
            s = jnp.where(causal, s, NEG)
            _online_update(h, s, _head(v, h), m_sc, l_sc, acc_sc)

    @pl.when(j == pl.num_programs(2) - 1)
    def _():
        _flash_finish(o_ref, l_sc, acc_sc)


def _fox_prompt(q, k, v, fq, fk):
    b, s, d = q.shape
    rows = FOX_ROWS
    n = s // rows
    return pl.pallas_call(
        functools.partial(_fox_prompt_kernel, rows=rows),
        grid=(b, n, n),
        in_specs=[pl.BlockSpec((1, rows, d), lambda bi, i, j: (bi, i, 0)),
                  pl.BlockSpec((1, rows, d), lambda bi, i, j: (bi, jnp.minimum(i, j), 0)),
                  pl.BlockSpec((1, rows, d), lambda bi, i, j: (bi, jnp.minimum(i, j), 0)),
                  pl.BlockSpec((1, rows, N_HEADS), lambda bi, i, j: (bi, i, 0)),
                  pl.BlockSpec((1, N_HEADS, rows), lambda bi, i, j: (bi, 0, jnp.minimum(i, j)))],
        out_specs=pl.BlockSpec((1, rows, d), lambda bi, i, j: (bi, i, 0)),
        out_shape=jax.ShapeDtypeStruct((b, s, d), F32),
        scratch_shapes=[pltpu.VMEM((N_HEADS, rows, 1), F32), pltpu.VMEM((N_HEADS, rows, 1), F32),
                        pltpu.VMEM((rows, d), F32)],
        compiler_params=_params(("arbitrary", "arbitrary", "arbitrary"), 48),
    )(q, k, v, fq, fk)


def _fox_sample_kernel(q_ref, kc_ref, vc_ref, kn_ref, vn_ref, fq_ref, fkc_ref, fkn_ref, o_ref,
                       m_sc, l_sc, acc_sc, *, t_new):
    j = pl.program_id(1)

    @pl.when(j == 0)
    def _():
        _flash_init(m_sc, l_sc, acc_sc)

    q = q_ref[0]
    fq = fq_ref[0]
    kc = kc_ref[0].astype(BF16)
    vc = vc_ref[0].astype(BF16)
    fkc = fkc_ref[0]
    for h in range(N_HEADS):
        s = _dot_nt(_head(q, h).astype(BF16), _head(kc, h)) * SCALE + (fq[:, h:h + 1] - fkc[h:h + 1, :])
        _online_update(h, s, _head(vc, h), m_sc, l_sc, acc_sc)

    @pl.when(j == pl.num_programs(1) - 1)
    def _():
        kn = kn_ref[0].astype(BF16)
        vn = vn_ref[0].astype(BF16)
        fkn = fkn_ref[0]
        tq = lax.broadcasted_iota(jnp.int32, (t_new, t_new), 0)
        tk = lax.broadcasted_iota(jnp.int32, (t_new, t_new), 1)
        causal = tk <= tq
        for h in range(N_HEADS):
            s = _dot_nt(_head(q, h).astype(BF16), _head(kn, h)) * SCALE + (fq[:, h:h + 1] - fkn[h:h + 1, :])
            s = jnp.where(causal, s, NEG)
            _online_update(h, s, _head(vn, h), m_sc, l_sc, acc_sc)
        _flash_finish(o_ref, l_sc, acc_sc)


def _fox_sample(q, k_new, v_new, k_cache, v_cache, fq, fk_cache, fk_new):
    b, t, d = q.shape
    p = k_cache.shape[1]
    rows = min(FOX_CACHE_ROWS, p)
    new = pl.BlockSpec((1, t, d), lambda i, j: (i, 0, 0))
    old = pl.BlockSpec((1, rows, d), lambda i, j: (i, j, 0))
    return pl.pallas_call(
        functools.partial(_fox_sample_kernel, t_new=t),
        grid=(b, p // rows),
        in_specs=[new, old, old, new, new,
                  pl.BlockSpec((1, t, N_HEADS), lambda i, j: (i, 0, 0)),
                  pl.BlockSpec((1, N_HEADS, rows), lambda i, j: (i, 0, j)),
                  pl.BlockSpec((1, N_HEADS, t), lambda i, j: (i, 0, 0))],
        out_specs=new,
        out_shape=jax.ShapeDtypeStruct((b, t, d), F32),
        scratch_shapes=[pltpu.VMEM((N_HEADS, t, 1), F32), pltpu.VMEM((N_HEADS, t, 1), F32),
                        pltpu.VMEM((t, d), F32)],
        compiler_params=_params(("arbitrary", "arbitrary"), 48),
    )(q, k_cache, v_cache, k_new, v_new, fq, fk_cache, fk_new)


def _mix_kernel(x_ref, ya_ref, yb_ref, yc_ref, g_ref, w_ref, lg_ref, lb_ref, o_ref, *, alpha):
    d_a = ya_ref.shape[-1]
    d_b = yb_ref.shape[-1]
    g = g_ref[...]
    y = jnp.concatenate([_rms_norm(ya_ref[...], g[:, :d_a]),
                         _rms_norm(yb_ref[...], g[:, d_a:d_a + d_b]),
                         _rms_norm(yc_ref[...], g[:, d_a + d_b:])], axis=1)
    z = alpha * x_ref[...] + jnp.dot(y.astype(BF16), w_ref[...], preferred_element_type=F32)
    o_ref[...] = _layer_norm(z, lg_ref[...], lb_ref[...])


def _mix(x2d, ya, yb, yc, g_mix, w_out, ln_g, ln_b, alpha):
    t, d = x2d.shape
    rows = min(MIX_ROWS, t)
    tok = lambda w: pl.BlockSpec((rows, w), lambda i: (i, 0))
    vec = _resident((1, d))
    return pl.pallas_call(
        functools.partial(_mix_kernel, alpha=alpha),
        grid=(t // rows,),
        in_specs=[tok(d), tok(ya.shape[1]), tok(yb.shape[1]), tok(yc.shape[1]), vec, _resident(w_out.shape),
                  vec, vec],
        out_specs=tok(d),
        out_shape=jax.ShapeDtypeStruct((t, d), F32),
        compiler_params=_params(("arbitrary",), 48),
    )(x2d, ya, yb, yc, g_mix, w_out, ln_g, ln_b)


def _ffn_kernel(x_ref, wu_ref, wd_ref, lg_ref, lb_ref, o_ref, xb_sc, acc_sc, *, alpha):
    j = pl.program_id(1)

    @pl.when(j == 0)
    def _():
        xb_sc[...] = x_ref[...].astype(BF16)
        acc_sc[...] = jnp.zeros(acc_sc.shape, F32)

    hid = jnp.maximum(jnp.dot(xb_sc[...], wu_ref[...], preferred_element_type=F32), 0.0)
    acc_sc[...] += jnp.dot((hid * hid).astype(BF16), wd_ref[...], preferred_element_type=F32)

    @pl.when(j == pl.num_programs(1) - 1)
    def _():
        o_ref[...] = _layer_norm(alpha * x_ref[...] + acc_sc[...], lg_ref[...], lb_ref[...])


def _ffn(x2d, w_up, w_down, ln_g, ln_b, alpha):
    t, d = x2d.shape
    d_ff = w_up.shape[1]
    rows, cols = min(FFN_ROWS, t), FFN_COLS
    vec = _resident((1, d))
    return pl.pallas_call(
        functools.partial(_ffn_kernel, alpha=alpha),
        grid=(t // rows, d_ff // cols),
        in_specs=[pl.BlockSpec((rows, d), lambda i, j: (i, 0)),
                  pl.BlockSpec((d, cols), lambda i, j: (0, j)),
                  pl.BlockSpec((cols, d), lambda i, j: (j, 0)),
                  vec, vec],
        out_specs=pl.BlockSpec((rows, d), lambda i, j: (i, 0)),
        out_shape=jax.ShapeDtypeStruct((t, d), F32),
        scratch_shapes=[pltpu.VMEM((rows, d), BF16), pltpu.VMEM((rows, d), F32)],
        compiler_params=_params(("arbitrary", "arbitrary"), 56),
    )(x2d, w_up, w_down, ln_g, ln_b)


def _cum_logf(logf_t, off):
    b, h, t = logf_t.shape
    t_pad = -(-t // LANES) * LANES
    x = logf_t.reshape(b * h, t)
    if t_pad != t:
        x = jnp.pad(x, ((0, 0), (0, t_pad - t)))
    return _cumsum_lanes(x, off.reshape(b * h, 1))[:, :t].reshape(b, h, t)


def _row(v):
    return v.reshape(1, -1)


def kernel(x_prompt, x_sample, state_conv, state_lru, cache_band_k, cache_band_v, cache_fox_k, cache_fox_v,
           cache_fox_logf, w_in, w_conv, b_conv, w_rg_a, b_rg_a, w_rg_x, b_rg_x, lru_lambda, rel_bias,
           b_forget, g_mix, w_out, ln1_g, ln1_b, w_up, w_down, ln2_g, ln2_b):
    depth = w_in.shape[0]
    bp, s, d = x_prompt.shape
    bs, ts, _ = x_sample.shape
    d_a = w_conv.shape[-1]
    alpha = float((2 * depth) ** 0.25)
    d_main = 2 * d_a + 6 * D_ATT
    widths = (2 * d_a,) + (D_ATT,) * 6 + (LANES,)
    n_band_p = min(BAND, s)
    n_cache_b = cache_band_k.shape[2]
    p_len = cache_fox_k.shape[2]

    xp = x_prompt.reshape(bp * s, d)
    xs = x_sample.reshape(bs * ts, d)
    outs = [[] for _ in range(14)]

    for l in range(depth):
        w_pad = jnp.concatenate(
            [w_in[l][:, :d_main], jnp.pad(w_in[l][:, d_main:], ((0, 0), (0, LANES - N_HEADS)))], axis=1).astype(BF16)
        bf_pad = jnp.pad(b_forget[l], (0, LANES - N_HEADS)).reshape(1, LANES)
        lru_w = (w_conv[l], _row(b_conv[l]), w_rg_a[l].astype(BF16), _row(b_rg_a[l]),
                 w_rg_x[l].astype(BF16), _row(b_rg_x[l]), _row(lru_lambda[l]))
        w_out_l = w_out[l].astype(BF16)
        w_up_l = w_up[l].astype(BF16)
        w_down_l = w_down[l].astype(BF16)

        def finish(x2d, ya, yb, yc):
            x1 = _mix(x2d, ya, yb, yc, _row(g_mix[l]), w_out_l, _row(ln1_g[l]), _row(ln1_b[l]), alpha)
            return _ffn(x1, w_up_l, w_down_l, _row(ln2_g[l]), _row(ln2_b[l]), alpha)

        a2, bq, bk, bv, cq, ck, cv, lf = _proj(xp, w_pad, bf_pad, widths)
        ya, buf, h_last = _rglru(a2.reshape(bp, s, 2 * d_a), jnp.zeros((bp, SUBLANES, d_a), F32),
                                 jnp.zeros((bp, 1, d_a), F32), *lru_w, reset_first=True)
        r3 = lambda z: z.reshape(bp, s, D_ATT)
        yb = _band_prompt(r3(bq), r3(bk), r3(bv), rel_bias[l])
        logf = lf[:, :N_HEADS].reshape(bp, s, N_HEADS)
        fk = _cum_logf(jnp.swapaxes(logf, 1, 2), jnp.zeros((bp, N_HEADS, 1), F32))
        yc = _fox_prompt(r3(cq), r3(ck), r3(cv), jnp.swapaxes(fk, 1, 2), fk)
        xp = finish(xp, ya.reshape(bp * s, d_a), yb.reshape(bp * s, D_ATT), yc.reshape(bp * s, D_ATT))
        h4 = lambda z, b_, t_: z.reshape(b_, t_, N_HEADS, HEAD_DIM)
        for dst, val in zip(outs[:7], (buf[:, SUBLANES - (CONV_W - 1):], h_last.reshape(bp, d_a),
                                       h4(bk, bp, s)[:, s - n_band_p:], h4(bv, bp, s)[:, s - n_band_p:],
                                       h4(ck, bp, s), h4(cv, bp, s), logf)):
            dst.append(val)

        a2, bq, bk, bv, cq, ck, cv, lf = _proj(xs, w_pad, bf_pad, widths)
        conv8 = jnp.pad(state_conv[l], ((0, 0), (SUBLANES - (CONV_W - 1), 0), (0, 0)))
        ya, buf, h_last = _rglru(a2.reshape(bs, ts, 2 * d_a), conv8, state_lru[l].reshape(bs, 1, d_a),
                                 *lru_w, reset_first=False)
        r3 = lambda z: z.reshape(bs, ts, D_ATT)
        yb = _band_sample(r3(bq), r3(bk), r3(bv), cache_band_k[l].reshape(bs, n_cache_b, D_ATT),
                          cache_band_v[l].reshape(bs, n_cache_b, D_ATT), rel_bias[l])
        logf = lf[:, :N_HEADS].reshape(bs, ts, N_HEADS)
        fk_cache = _cum_logf(jnp.swapaxes(cache_fox_logf[l], 1, 2), jnp.zeros((bs, N_HEADS, 1), F32))
        fk_new = _cum_logf(jnp.swapaxes(logf, 1, 2), fk_cache[:, :, p_len - 1:])
        yc = _fox_sample(r3(cq), r3(ck), r3(cv), cache_fox_k[l].reshape(bs, p_len, D_ATT),
                         cache_fox_v[l].reshape(bs, p_len, D_ATT), jnp.swapaxes(fk_new, 1, 2), fk_cache, fk_new)
        xs = finish(xs, ya.reshape(bs * ts, d_a), yb.reshape(bs * ts, D_ATT), yc.reshape(bs * ts, D_ATT))
        for dst, val in zip(outs[7:], (buf[:, SUBLANES - (CONV_W - 1):], h_last.reshape(bs, d_a),
                                       h4(bk, bs, ts), h4(bv, bs, ts), h4(ck, bs, ts), h4(cv, bs, ts), logf)):
            dst.append(val)

    stacked = [jnp.stack(o, axis=0) for o in outs]
    return (xp.reshape(bp, s, d), xs.reshape(bs, ts, d), *stacked)
```

```python
import functools

import numpy as np
import jax
import jax.numpy as jnp
from jax import lax
from jax.experimental import pallas as pl
from jax.experimental.pallas import tpu as pltpu

F32 = jnp.float32
BF16 = jnp.bfloat16

HEAD_DIM = 128
CHUNK = 64
N_PREV_CHUNKS = 8
BAND = N_PREV_CHUNKS * CHUNK
REL_CLIP = 256
CONV_W = 4
LRU_C = 8.0
LN_EPS = 1e-5
NEG = -1e30
SCALE = HEAD_DIM ** -0.5
LOG2E = float(np.log2(np.e))
N_HEADS = 4
D_ATT = N_HEADS * HEAD_DIM
LANES = 128
SUBLANES = 8
MIB = 1 << 20
N_SPLIT = 3
REL_ROW = 2 * BAND

PROJ_ROWS = 256
LRU_ROWS = 256
BAND_ROWS = 256
FOX_ROWS = 512
FOX_CACHE_ROWS = 2048
MIX_ROWS = 512
FFN_ROWS = 512
FFN_COLS = 1024


def _params(semantics, vmem_mib):
    return pltpu.CompilerParams(dimension_semantics=semantics, vmem_limit_bytes=vmem_mib * MIB)


def _resident(shape):
    return pl.BlockSpec(shape, lambda *_: (0,) * len(shape), pipeline_mode=pl.Buffered(1))


def _layer_block(layer, shape):
    return pl.BlockSpec((None,) + tuple(shape), lambda *_: (layer,) + (0,) * len(shape),
                        pipeline_mode=pl.Buffered(1))


def _log_sigmoid(z):
    return jnp.minimum(z, 0.0) - jnp.log1p(jnp.exp(-jnp.abs(z)))


def _softplus(z):
    return jnp.maximum(z, 0.0) + jnp.log1p(jnp.exp(-jnp.abs(z)))


def _gelu_tanh(x):
    c = np.sqrt(2.0 / np.pi).astype(np.float32)
    return 0.5 * x * (1.0 + jnp.tanh(c * (x + 0.044715 * (x * x * x))))


def _layer_norm(z, g, b):
    mu = jnp.mean(z, axis=-1, keepdims=True)
    zc = z - mu
    var = jnp.mean(zc * zc, axis=-1, keepdims=True)
    return zc * lax.rsqrt(var + LN_EPS) * g + b


def _rms_norm(y, g):
    return y * lax.rsqrt(jnp.mean(y * y, axis=-1, keepdims=True) + LN_EPS) * g


def _dot_nt(a, b):
    return lax.dot_general(a, b, (((1,), (1,)), ((), ())), preferred_element_type=F32)


def _head(x, h):
    return x[:, h * HEAD_DIM:(h + 1) * HEAD_DIM]


def _head_rows(h, rows):
    return pl.ds(h, rows, stride=N_HEADS)


def _interleaved(a):
    return a.reshape(a.shape[:-3] + (a.shape[-3] * N_HEADS, HEAD_DIM))


def _cumsum_rows(x):
    row = lax.broadcasted_iota(jnp.int32, x.shape, 0)
    shift = 1
    while shift < x.shape[0]:
        x = jnp.where(row >= shift, x + pltpu.roll(x, shift, 0), x)
        shift *= 2
    return x


def _split_bf16(x):
    pieces = []
    for _ in range(N_SPLIT):
        p = x.astype(BF16).astype(F32)
        pieces.append(p)
        x = x - p
    return pieces


def _proj_kernel(*refs, widths, fox_aug, blocks_per_batch):
    if fox_aug:
        (x_ref, w_ref, bf_ref, _, _, a2_ref, bq_ref, bk_ref, bv_ref, lf_ref,
         qa_ref, ka_ref, vb_ref, k5_ref, v5_ref, f_sc) = refs
    else:
        x_ref, w_ref, bf_ref, a2_ref, bq_ref, bk_ref, bv_ref, lf_ref, cq_ref, ck_ref, cv_ref = refs
    xb = x_ref[...].astype(BF16)
    offs = np.concatenate([[0], np.cumsum(widths)])

    def group(g):
        return jnp.dot(xb, w_ref[:, int(offs[g]):int(offs[g + 1])], preferred_element_type=F32)

    a2_ref[...] = group(0)
    bq_ref[...] = group(1)
    bk_ref[...] = group(2)
    bv_ref[...] = group(3)
    logf = _log_sigmoid(group(7) + bf_ref[...])
    lf_ref[...] = logf
    if not fox_aug:
        cq_ref[...] = group(4)
        ck_ref[...] = group(5)
        cv_ref[...] = group(6)
        return

    cq, ck, cv = group(4), group(5), group(6)
    vb_ref[...] = cv.astype(BF16)

    @pl.when(pl.program_id(0) % blocks_per_batch == 0)
    def _():
        f_sc[...] = jnp.zeros(f_sc.shape, F32)

    rows = logf.shape[0]
    cum = _cumsum_rows(logf) + f_sc[...]
    f_sc[...] = cum[rows - 1:rows]
    cum2 = cum * LOG2E
    lane = lax.broadcasted_iota(jnp.int32, (rows, HEAD_DIM), 1)
    for h in range(N_HEADS):
        pieces = _split_bf16(jnp.broadcast_to(cum2[:, h:h + 1], (rows, HEAD_DIM)))
        q_extra = jnp.where(lane < 2 * N_SPLIT, 1.0, 0.0)
        k_extra = jnp.where(lane < N_SPLIT, 1.0, 0.0)
        for n, piece in enumerate(pieces):
            q_extra = jnp.where(lane == n, piece, q_extra)
            k_extra = jnp.where(lane == N_SPLIT + n, -piece, k_extra)
        qa_ref[h, :, :HEAD_DIM] = (_head(cq, h) * (SCALE * LOG2E)).astype(BF16)
        qa_ref[h, :, HEAD_DIM:] = q_extra.astype(BF16)
        ka_ref[h, :, :HEAD_DIM] = _head(ck, h).astype(BF16)
        ka_ref[h, :, HEAD_DIM:] = k_extra.astype(BF16)
        k5_ref[_head_rows(h, rows), :] = _head(ck, h)
        v5_ref[_head_rows(h, rows), :] = _head(cv, h)


def _proj_sample(x2d, w_all, layer, bf_pad, widths):
    t, d = x2d.shape
    n = w_all.shape[2]
    rows = min(PROJ_ROWS, t)
    order = (0, 1, 2, 3, 7, 4, 5, 6)
    return pl.pallas_call(
        functools.partial(_proj_kernel, widths=widths, fox_aug=False, blocks_per_batch=1),
        grid=(t // rows,),
        in_specs=[pl.BlockSpec((rows, d), lambda i: (i, 0)), _layer_block(layer, (d, n)),
                  _resident((1, widths[-1]))],
        out_specs=[pl.BlockSpec((rows, widths[g]), lambda i: (i, 0)) for g in order],
        out_shape=[jax.ShapeDtypeStruct((t, widths[g]), F32) for g in order],
        compiler_params=_params(("arbitrary",), 48),
        name="proj_sample",
    )(x2d, w_all, bf_pad)


def _proj_prompt(x2d, w_all, layer, bf_pad, widths, k5_all, v5_all, seq):
    t, d = x2d.shape
    n = w_all.shape[2]
    rows = PROJ_ROWS
    bpb = seq // rows
    tok = lambda w, dt: (pl.BlockSpec((rows, w), lambda i: (i, 0)), jax.ShapeDtypeStruct((t, w), dt))
    aug = (pl.BlockSpec((N_HEADS, rows, 2 * HEAD_DIM), lambda i: (0, i, 0)),
           jax.ShapeDtypeStruct((N_HEADS, t, 2 * HEAD_DIM), BF16))
    kv5 = lambda a: (pl.BlockSpec((None, None, rows * N_HEADS, HEAD_DIM),
                                  lambda i: (layer, i // bpb, i % bpb, 0)),
                     jax.ShapeDtypeStruct(a.shape, a.dtype))
    outs = [tok(widths[0], F32), tok(D_ATT, F32), tok(D_ATT, F32), tok(D_ATT, F32), tok(widths[-1], F32),
            aug, aug, tok(D_ATT, BF16), kv5(k5_all), kv5(v5_all)]
    return pl.pallas_call(
        functools.partial(_proj_kernel, widths=widths, fox_aug=True, blocks_per_batch=bpb),
        grid=(t // rows,),
        in_specs=[pl.BlockSpec((rows, d), lambda i: (i, 0)), _layer_block(layer, (d, n)),
                  _resident((1, widths[-1])),
                  pl.BlockSpec(memory_space=pl.ANY), pl.BlockSpec(memory_space=pl.ANY)],
        out_specs=[o[0] for o in outs],
        out_shape=[o[1] for o in outs],
        scratch_shapes=[pltpu.VMEM((1, widths[-1]), F32)],
        input_output_aliases={3: 8, 4: 9},
        compiler_params=_params(("arbitrary",), 52),
        name="proj_prompt",
    )(x2d, w_all, bf_pad, k5_all, v5_all)


def _rglru_kernel(ax_ref, ag_ref, cs_ref, h0_ref, wc_ref, bc_ref, wra_ref, bra_ref, wrx_ref, brx_ref,
                  lam_ref, y_ref, buf_ref, hl_ref, h_sc, tail_sc, *, rows, n_blocks, reset_first):
    t = pl.program_id(1)
    d_a = ax_ref.shape[-1]

    @pl.when(t == 0)
    def _():
        h_sc[...] = h0_ref[0]
        tail_sc[...] = cs_ref[0]

    xa = ax_ref[0]
    ext = jnp.concatenate([tail_sc[...], xa], axis=0)
    xc = bc_ref[...] + wc_ref[CONV_W - 1:CONV_W, :] * xa
    for j in range(1, CONV_W):
        shifted = pltpu.roll(ext, j, 0)[SUBLANES:SUBLANES + rows]
        xc = xc + wc_ref[CONV_W - 1 - j:CONV_W - j, :] * shifted
    tail_sc[...] = xa[rows - SUBLANES:rows]

    xcb = xc.astype(BF16)

    def gate(w_ref, b_ref):
        parts = [jnp.dot(xcb[:, n * HEAD_DIM:(n + 1) * HEAD_DIM], w_ref[n], preferred_element_type=F32)
                 for n in range(n_blocks)]
        return jax.nn.sigmoid(jnp.concatenate(parts, axis=1) + b_ref[...])

    r = gate(wra_ref, bra_ref)
    i_gate = gate(wrx_ref, brx_ref)
    log_a = (-LRU_C) * r * _softplus(-lam_ref[...])
    a = jnp.exp(log_a)
    mult = jnp.sqrt(1.0 - a * a)
    row = lax.broadcasted_iota(jnp.int32, (rows, d_a), 0)
    if reset_first:
        mult = jnp.where(jnp.logical_and(row == 0, t == 0), 1.0, mult)
    u = mult * (i_gate * xc)

    shift = 1
    while shift < rows:
        a_prev = pltpu.roll(a, shift, 0)
        u_prev = pltpu.roll(u, shift, 0)
        live = row >= shift
        u = jnp.where(live, a * u_prev + u, u)
        a = jnp.where(live, a * a_prev, a)
        shift *= 2
    h = u + a * h_sc[...]
    h_last = h[rows - 1:rows]
    h_sc[...] = h_last

    y_ref[0] = h * _gelu_tanh(ag_ref[0])

    @pl.when(t == pl.num_programs(1) - 1)
    def _():
        buf_ref[0] = xa[rows - SUBLANES:rows]
        hl_ref[0] = h_last


def _rglru(a2, conv_state8, h0, w_conv, b_conv, w_ra, b_ra, w_rx, b_rx, lam, *, reset_first):
    b, t, d2 = a2.shape
    d_a = d2 // 2
    rows = min(LRU_ROWS, t)
    n_blocks = d_a // HEAD_DIM
    row_vec = _resident((1, d_a))
    gate_w = _resident((n_blocks, HEAD_DIM, HEAD_DIM))
    kern = functools.partial(_rglru_kernel, rows=rows, n_blocks=n_blocks, reset_first=reset_first)
    return pl.pallas_call(
        kern,
        grid=(b, t // rows),
        in_specs=[pl.BlockSpec((1, rows, d_a), lambda i, j: (i, j, 0)),
                  pl.BlockSpec((1, rows, d_a), lambda i, j: (i, j, 1)),
                  pl.BlockSpec((1, SUBLANES, d_a), lambda i, j: (i, 0, 0)),
                  pl.BlockSpec((1, 1, d_a), lambda i, j: (i, 0, 0)),
                  _resident((CONV_W, d_a)), row_vec, gate_w, row_vec, gate_w, row_vec, row_vec],
        out_specs=[pl.BlockSpec((1, rows, d_a), lambda i, j: (i, j, 0)),
                   pl.BlockSpec((1, SUBLANES, d_a), lambda i, j: (i, 0, 0)),
                   pl.BlockSpec((1, 1, d_a), lambda i, j: (i, 0, 0))],
        out_shape=[jax.ShapeDtypeStruct((b, t, d_a), F32),
                   jax.ShapeDtypeStruct((b, SUBLANES, d_a), F32),
                   jax.ShapeDtypeStruct((b, 1, d_a), F32)],
        scratch_shapes=[pltpu.VMEM((1, d_a), F32), pltpu.VMEM((SUBLANES, d_a), F32)],
        compiler_params=_params(("arbitrary", "arbitrary"), 48),
        name="rglru_reset" if reset_first else "rglru_carry",
    )(a2, a2, conv_state8, h0, w_conv, b_conv, w_ra, b_ra, w_rx, b_rx, lam)


def _cumsum_kernel(x_ref, off_ref, o_ref):
    x = x_ref[...]
    n = x.shape[1]
    lane = lax.broadcasted_iota(jnp.int32, x.shape, 1)
    shift = 1
    while shift < n:
        prev = pltpu.roll(x, shift, 1)
        x = jnp.where(lane >= shift, x + prev, x)
        shift *= 2
    o_ref[...] = x + off_ref[...]


def _cumsum_lanes(x, off):
    r, n = x.shape
    return pl.pallas_call(
        _cumsum_kernel,
        grid=(1,),
        in_specs=[pl.BlockSpec((r, n), lambda i: (0, 0)), pl.BlockSpec((r, 1), lambda i: (0, 0))],
        out_specs=pl.BlockSpec((r, n), lambda i: (0, 0)),
        out_shape=jax.ShapeDtypeStruct((r, n), F32),
        compiler_params=_params(("arbitrary",), 32),
        name="cumsum_lanes",
    )(x, off)


def _softmax_pv(s, v_h):
    m = jnp.max(s, axis=-1, keepdims=True)
    p = jnp.exp2(s - m)
    den = jnp.sum(p, axis=-1, keepdims=True)
    return jnp.dot(p.astype(BF16), v_h, preferred_element_type=F32) / den


def _flash_step(scores, values, m_sc, l_sc, acc_sc):
    prev = [(m_sc[h], l_sc[h], acc_sc[h]) for h in range(N_HEADS)]
    new = []
    for (m_prev, l_prev, acc_prev), s, v_h in zip(prev, scores, values):
        m_new = jnp.maximum(m_prev, jnp.max(s, axis=-1, keepdims=True))
        alpha = jnp.exp2(m_prev - m_new)
        p = jnp.exp2(s - m_new)
        new.append((m_new, alpha * l_prev + jnp.sum(p, axis=-1, keepdims=True),
                    alpha * acc_prev + jnp.dot(p.astype(BF16), v_h, preferred_element_type=F32)))
    for h, (m_new, l_new, acc_new) in enumerate(new):
        m_sc[h] = m_new
        l_sc[h] = l_new
        acc_sc[h] = acc_new


def _flash_init(m_sc, l_sc, acc_sc):
    m_sc[...] = jnp.full(m_sc.shape, NEG, F32)
    l_sc[...] = jnp.zeros(l_sc.shape, F32)
    acc_sc[...] = jnp.zeros(acc_sc.shape, F32)


def _flash_finish(o_ref, l_sc, acc_sc):
    for h in range(N_HEADS):
        o_ref[0, :, h * HEAD_DIM:(h + 1) * HEAD_DIM] = acc_sc[h] / l_sc[h]


def _flash_scratch(rows):
    return [pltpu.VMEM((N_HEADS, rows, 1), F32), pltpu.VMEM((N_HEADS, rows, 1), F32),
            pltpu.VMEM((N_HEADS, rows, HEAD_DIM), F32)]


def _rel_row(rel_table):
    edge = rel_table[:, 2 * REL_CLIP:]
    row = jnp.concatenate([jnp.broadcast_to(edge, (N_HEADS, BAND - REL_CLIP)), rel_table[:, ::-1],
                           jnp.broadcast_to(edge, (N_HEADS, REL_ROW - BAND - REL_CLIP - 1))], axis=1)
    return row.reshape(N_HEADS, 1, REL_ROW).astype(F32)


def _rel_tile(row_ref, h, rows, cols):
    tile = pltpu.roll(jnp.broadcast_to(row_ref[h], (rows, REL_ROW)), 0, 1, stride=1, stride_axis=0)
    return tile[:, :cols] * LOG2E


def _band_prompt_kernel(q_ref, k0_ref, k1_ref, k2_ref, v0_ref, v1_ref, v2_ref, row_ref, o_ref, bias_sc, *, rows):
    i = pl.program_id(1)

    @pl.when(jnp.logical_and(pl.program_id(0) == 0, i == 0))
    def _():
        qc = lax.broadcasted_iota(jnp.int32, (rows, 3 * rows), 0) // CHUNK
        kc = lax.broadcasted_iota(jnp.int32, (rows, 3 * rows), 1) // CHUNK - (2 * rows) // CHUNK
        visible = jnp.logical_and(kc <= qc, kc >= qc - N_PREV_CHUNKS)
        for h in range(N_HEADS):
            bias_sc[h] = jnp.where(visible, _rel_tile(row_ref, h, rows, 3 * rows), NEG)

    def attend(clip_start):
        q = q_ref[0]
        k = jnp.concatenate([k0_ref[0], k1_ref[0], k2_ref[0]], axis=0).astype(BF16)
        v = jnp.concatenate([v0_ref[0], v1_ref[0], v2_ref[0]], axis=0).astype(BF16)
        for h in range(N_HEADS):
            s = _dot_nt((_head(q, h) * (SCALE * LOG2E)).astype(BF16), _head(k, h)) + bias_sc[h]
            if clip_start:
                col = lax.broadcasted_iota(jnp.int32, (rows, 3 * rows), 1)
                s = jnp.where(col >= (2 - i) * rows, s, NEG)
            o_ref[0, :, h * HEAD_DIM:(h + 1) * HEAD_DIM] = _softmax_pv(s, _head(v, h))

    @pl.when(i < 2)
    def _():
        attend(True)

    @pl.when(i >= 2)
    def _():
        attend(False)


def _band_prompt(q, k, v, rel_row):
    b, s, d = q.shape
    rows = BAND_ROWS
    assert BAND == 2 * rows and s % rows == 0
    cur = pl.BlockSpec((1, rows, d), lambda i, j: (i, j, 0))
    prev1 = pl.BlockSpec((1, rows, d), lambda i, j: (i, jnp.maximum(j - 1, 0), 0))
    prev2 = pl.BlockSpec((1, rows, d), lambda i, j: (i, jnp.maximum(j - 2, 0), 0))
    return pl.pallas_call(
        functools.partial(_band_prompt_kernel, rows=rows),
        grid=(b, s // rows),
        in_specs=[cur, prev2, prev1, cur, prev2, prev1, cur, _resident(rel_row.shape)],
        out_specs=cur,
        out_shape=jax.ShapeDtypeStruct((b, s, d), F32),
        scratch_shapes=[pltpu.VMEM((N_HEADS, rows, 3 * rows), F32)],
        compiler_params=_params(("arbitrary", "arbitrary"), 48),
        name="band_prompt",
    )(q, k, k, k, v, v, v, rel_row)


def _band_sample_kernel(q_ref, kn_ref, vn_ref, kc_ref, vc_ref, row_ref, o_ref, bias_sc, *, t_new, n_cache):
    @pl.when(pl.program_id(0) == 0)
    def _():
        for h in range(N_HEADS):
            bias_sc[h] = _rel_tile(row_ref, h, t_new, n_cache + t_new)

    q = q_ref[0]
    kn = kn_ref[0].astype(BF16)
    vn = vn_ref[0].astype(BF16)
    for h in range(N_HEADS):
        cached = _head_rows(h, n_cache)
        k = jnp.concatenate([kc_ref[cached, :].astype(BF16), _head(kn, h)], axis=0)
        v = jnp.concatenate([vc_ref[cached, :].astype(BF16), _head(vn, h)], axis=0)
        s = _dot_nt((_head(q, h) * (SCALE * LOG2E)).astype(BF16), k) + bias_sc[h]
        o_ref[0, :, h * HEAD_DIM:(h + 1) * HEAD_DIM] = _softmax_pv(s, v)


def _band_sample(q, k_new, v_new, k_cache, v_cache, layer, rel_row):
    b, t, d = q.shape
    n_c = k_cache.shape[2] // N_HEADS
    assert n_c == BAND
    new = pl.BlockSpec((1, t, d), lambda i: (i, 0, 0))
    old = pl.BlockSpec((None, None, n_c * N_HEADS, HEAD_DIM), lambda i: (layer, i, 0, 0))
    return pl.pallas_call(
        functools.partial(_band_sample_kernel, t_new=t, n_cache=n_c),
        grid=(b,),
        in_specs=[new, new, new, old, old, _resident(rel_row.shape)],
        out_specs=new,
        out_shape=jax.ShapeDtypeStruct((b, t, d), F32),
        scratch_shapes=[pltpu.VMEM((N_HEADS, t, n_c + t), F32)],
        compiler_params=_params(("arbitrary",), 32),
        name="band_sample",
    )(q, k_new, v_new, k_cache, v_cache, rel_row)


def _fox_prompt_kernel(qa_ref, ka_ref, v_ref, o_ref, m_sc, l_sc, acc_sc, *, rows):
    i = pl.program_id(1)
    j = pl.program_id(2)

    @pl.when(j == 0)
    def _():
        _flash_init(m_sc, l_sc, acc_sc)

    def attend(diagonal):
        v = v_ref[...]
        scores = [_dot_nt(qa_ref[h], ka_ref[h]) for h in range(N_HEADS)]
        if diagonal:
            causal = (lax.broadcasted_iota(jnp.int32, (rows, rows), 1)
                      <= lax.broadcasted_iota(jnp.int32, (rows, rows), 0))
            scores = [jnp.where(causal, s, NEG) for s in scores]
        _flash_step(scores, [_head(v, h) for h in range(N_HEADS)], m_sc, l_sc, acc_sc)

    @pl.when(j < i)
    def _():
        attend(False)

    @pl.when(j == i)
    def _():
        attend(True)
        _flash_finish(o_ref, l_sc, acc_sc)


def _fox_prompt(qa, ka, v, batch):
    _, t, da = qa.shape
    d = v.shape[1]
    rows = FOX_ROWS
    n = t // batch // rows
    return pl.pallas_call(
        functools.partial(_fox_prompt_kernel, rows=rows),
        grid=(batch, n, n),
        in_specs=[pl.BlockSpec((N_HEADS, rows, da), lambda b, i, j: (0, b * n + i, 0)),
                  pl.BlockSpec((N_HEADS, rows, da), lambda b, i, j: (0, b * n + jnp.minimum(i, j), 0)),
                  pl.BlockSpec((rows, d), lambda b, i, j: (b * n + jnp.minimum(i, j), 0))],
        out_specs=pl.BlockSpec((1, rows, d), lambda b, i, j: (b, i, 0)),
        out_shape=jax.ShapeDtypeStruct((batch, t // batch, d), F32),
        scratch_shapes=_flash_scratch(rows),
        compiler_params=_params(("arbitrary", "arbitrary", "arbitrary"), 48),
        name="fox_prompt",
    )(qa, ka, v)


def _fox_sample_kernel(q_ref, kc_ref, vc_ref, kn_ref, vn_ref, fq_ref, fkc_ref, fkn_ref, o_ref,
                       m_sc, l_sc, acc_sc, *, t_new):
    j = pl.program_id(1)

    @pl.when(j == 0)
    def _():
        _flash_init(m_sc, l_sc, acc_sc)

    q = q_ref[0]
    qs = [(_head(q, h) * (SCALE * LOG2E)).astype(BF16) for h in range(N_HEADS)]
    fq = fq_ref[0] * LOG2E

    def bias(fk, h):
        return fq[:, h:h + 1] - fk[h:h + 1, :] * LOG2E

    fkc = fkc_ref[0]
    cached = [_head_rows(h, fkc.shape[1]) for h in range(N_HEADS)]
    _flash_step([_dot_nt(qs[h], kc_ref[cached[h], :].astype(BF16)) + bias(fkc, h) for h in range(N_HEADS)],
                [vc_ref[cached[h], :].astype(BF16) for h in range(N_HEADS)], m_sc, l_sc, acc_sc)

    @pl.when(j == pl.num_programs(1) - 1)
    def _():
        kn = kn_ref[0].astype(BF16)
        vn = vn_ref[0].astype(BF16)
        fkn = fkn_ref[0]
        causal = (lax.broadcasted_iota(jnp.int32, (t_new, t_new), 1)
                  <= lax.broadcasted_iota(jnp.int32, (t_new, t_new), 0))
        _flash_step([jnp.where(causal, _dot_nt(qs[h], _head(kn, h)) + bias(fkn, h), NEG) for h in range(N_HEADS)],
                    [_head(vn, h) for h in range(N_HEADS)], m_sc, l_sc, acc_sc)
        _flash_finish(o_ref, l_sc, acc_sc)


def _fox_sample(q, k_new, v_new, k_cache, v_cache, layer, fq, fk_cache, fk_new):
    b, t, d = q.shape
    p = k_cache.shape[2] // N_HEADS
    rows = min(FOX_CACHE_ROWS, p)
    new = pl.BlockSpec((1, t, d), lambda i, j: (i, 0, 0))
    old = pl.BlockSpec((None, None, rows * N_HEADS, HEAD_DIM), lambda i, j: (layer, i, j, 0))
    return pl.pallas_call(
        functools.partial(_fox_sample_kernel, t_new=t),
        grid=(b, p // rows),
        in_specs=[new, old, old, new, new,
                  pl.BlockSpec((1, t, N_HEADS), lambda i, j: (i, 0, 0)),
                  pl.BlockSpec((1, N_HEADS, rows), lambda i, j: (i, 0, j)),
                  pl.BlockSpec((1, N_HEADS, t), lambda i, j: (i, 0, 0))],
        out_specs=new,
        out_shape=jax.ShapeDtypeStruct((b, t, d), F32),
        scratch_shapes=_flash_scratch(t),
        compiler_params=_params(("arbitrary", "arbitrary"), 48),
        name="fox_sample",
    )(q, k_cache, v_cache, k_new, v_new, fq, fk_cache, fk_new)


def _mix_kernel(x_ref, ya_ref, yb_ref, yc_ref, g_ref, w_ref, lg_ref, lb_ref, o_ref, *, alpha):
    d_a = ya_ref.shape[-1]
    d_b = yb_ref.shape[-1]
    g = g_ref[...]
    y = jnp.concatenate([_rms_norm(ya_ref[...], g[:, :d_a]),
                         _rms_norm(yb_ref[...], g[:, d_a:d_a + d_b]),
                         _rms_norm(yc_ref[...], g[:, d_a + d_b:])], axis=1)
    z = alpha * x_ref[...] + jnp.dot(y.astype(BF16), w_ref[...], preferred_element_type=F32)
    o_ref[...] = _layer_norm(z, lg_ref[...], lb_ref[...])


def _mix(x2d, ya, yb, yc, g_mix, w_out_all, layer, ln_g, ln_b, alpha):
    t, d = x2d.shape
    rows = min(MIX_ROWS, t)
    tok = lambda w: pl.BlockSpec((rows, w), lambda i: (i, 0))
    vec = _resident((1, d))
    return pl.pallas_call(
        functools.partial(_mix_kernel, alpha=alpha),
        grid=(t // rows,),
        in_specs=[tok(d), tok(ya.shape[1]), tok(yb.shape[1]), tok(yc.shape[1]), vec,
                  _layer_block(layer, w_out_all.shape[1:]), vec, vec],
        out_specs=tok(d),
        out_shape=jax.ShapeDtypeStruct((t, d), F32),
        compiler_params=_params(("arbitrary",), 48),
        name="mix",
    )(x2d, ya, yb, yc, g_mix, w_out_all, ln_g, ln_b)


def _ffn_kernel(x_ref, wu_ref, wd_ref, lg_ref, lb_ref, o_ref, xb_sc, acc_sc, *, alpha):
    j = pl.program_id(1)

    @pl.when(j == 0)
    def _():
        xb_sc[...] = x_ref[...].astype(BF16)
        acc_sc[...] = jnp.zeros(acc_sc.shape, F32)

    hid = jnp.maximum(jnp.dot(xb_sc[...], wu_ref[...], preferred_element_type=F32), 0.0)
    acc_sc[...] += jnp.dot((hid * hid).astype(BF16), wd_ref[...], preferred_element_type=F32)

    @pl.when(j == pl.num_programs(1) - 1)
    def _():
        o_ref[...] = _layer_norm(alpha * x_ref[...] + acc_sc[...], lg_ref[...], lb_ref[...])


def _ffn(x2d, w_up_all, w_down_all, layer, ln_g, ln_b, alpha):
    t, d = x2d.shape
    d_ff = w_up_all.shape[2]
    rows, cols = min(FFN_ROWS, t), FFN_COLS
    vec = _resident((1, d))
    return pl.pallas_call(
        functools.partial(_ffn_kernel, alpha=alpha),
        grid=(t // rows, d_ff // cols),
        in_specs=[pl.BlockSpec((rows, d), lambda i, j: (i, 0)),
                  pl.BlockSpec((None, d, cols), lambda i, j: (layer, 0, j)),
                  pl.BlockSpec((None, cols, d), lambda i, j: (layer, j, 0)),
                  vec, vec],
        out_specs=pl.BlockSpec((rows, d), lambda i, j: (i, 0)),
        out_shape=jax.ShapeDtypeStruct((t, d), F32),
        scratch_shapes=[pltpu.VMEM((rows, d), BF16), pltpu.VMEM((rows, d), F32)],
        compiler_params=_params(("arbitrary", "arbitrary"), 56),
        name="ffn",
    )(x2d, w_up_all, w_down_all, ln_g, ln_b)


def _cum_logf(logf_t, off):
    b, h, t = logf_t.shape
    t_pad = -(-t // LANES) * LANES
    x = logf_t.reshape(b * h, t)
    if t_pad != t:
        x = jnp.pad(x, ((0, 0), (0, t_pad - t)))
    return _cumsum_lanes(x, off.reshape(b * h, 1))[:, :t].reshape(b, h, t)


def _row(v):
    return v.reshape(1, -1)


def kernel(x_prompt, x_sample, state_conv, state_lru, cache_band_k, cache_band_v, cache_fox_k, cache_fox_v,
           cache_fox_logf, w_in, w_conv, b_conv, w_rg_a, b_rg_a, w_rg_x, b_rg_x, lru_lambda, rel_bias,
           b_forget, g_mix, w_out, ln1_g, ln1_b, w_up, w_down, ln2_g, ln2_b):
    depth = w_in.shape[0]
    bp, s, d = x_prompt.shape
    bs, ts, _ = x_sample.shape
    d_a = w_conv.shape[-1]
    alpha = float((2 * depth) ** 0.25)
    d_main = 2 * d_a + 6 * D_ATT
    widths = (2 * d_a,) + (D_ATT,) * 6 + (LANES,)
    n_band_p = min(BAND, s)
    p_len = cache_fox_k.shape[2]

    w_in_b = jnp.concatenate([w_in[:, :, :d_main],
                              jnp.pad(w_in[:, :, d_main:], ((0, 0), (0, 0), (0, LANES - N_HEADS)))],
                             axis=2).astype(BF16)
    w_ra_b, w_rx_b = w_rg_a.astype(BF16), w_rg_x.astype(BF16)
    w_out_b, w_up_b, w_down_b = w_out.astype(BF16), w_up.astype(BF16), w_down.astype(BF16)

    xp = x_prompt.reshape(bp * s, d)
    xs = x_sample.reshape(bs * ts, d)
    outs = [[] for _ in range(12)]
    k5 = jnp.zeros((depth, bp, s * N_HEADS, HEAD_DIM), F32)
    v5 = jnp.zeros((depth, bp, s * N_HEADS, HEAD_DIM), F32)
    band_kc, band_vc = _interleaved(cache_band_k), _interleaved(cache_band_v)
    fox_kc, fox_vc = _interleaved(cache_fox_k), _interleaved(cache_fox_v)

    for l in range(depth):
        bf_pad = jnp.pad(b_forget[l], (0, LANES - N_HEADS)).reshape(1, LANES)
        lru_w = (w_conv[l], _row(b_conv[l]), w_ra_b[l], _row(b_rg_a[l]), w_rx_b[l], _row(b_rg_x[l]),
                 _row(lru_lambda[l]))
        rel_row = _rel_row(rel_bias[l])

        def finish(x2d, ya, yb, yc):
            x1 = _mix(x2d, ya, yb, yc, _row(g_mix[l]), w_out_b, l, _row(ln1_g[l]), _row(ln1_b[l]), alpha)
            return _ffn(x1, w_up_b, w_down_b, l, _row(ln2_g[l]), _row(ln2_b[l]), alpha)

        a2, bq, bk, bv, lf, qa, ka, vb, k5, v5 = _proj_prompt(xp, w_in_b, l, bf_pad, widths, k5, v5, s)
        ya, buf, h_last = _rglru(a2.reshape(bp, s, 2 * d_a), jnp.zeros((bp, SUBLANES, d_a), F32),
                                 jnp.zeros((bp, 1, d_a), F32), *lru_w, reset_first=True)
        r3 = lambda z: z.reshape(bp, s, D_ATT)
        yb = _band_prompt(r3(bq), r3(bk), r3(bv), rel_row)
        yc = _fox_prompt(qa, ka, vb, bp)
        xp = finish(xp, ya.reshape(bp * s, d_a), yb.reshape(bp * s, D_ATT), yc.reshape(bp * s, D_ATT))
        h4 = lambda z, b_, t_: z.reshape(b_, t_, N_HEADS, HEAD_DIM)
        newest = lambda z: h4(r3(z)[:, s - n_band_p:], bp, n_band_p)
        for dst, val in zip(outs[:5], (buf[:, SUBLANES - (CONV_W - 1):], h_last.reshape(bp, d_a),
                                       newest(bk), newest(bv),
                                       lf[:, :N_HEADS].reshape(bp, s, N_HEADS))):
            dst.append(val)

        a2, bq, bk, bv, lf, cq, ck, cv = _proj_sample(xs, w_in_b, l, bf_pad, widths)
        conv8 = jnp.pad(state_conv[l], ((0, 0), (SUBLANES - (CONV_W - 1), 0), (0, 0)))
        ya, buf, h_last = _rglru(a2.reshape(bs, ts, 2 * d_a), conv8, state_lru[l].reshape(bs, 1, d_a),
                                 *lru_w, reset_first=False)
        r3 = lambda z: z.reshape(bs, ts, D_ATT)
        yb = _band_sample(r3(bq), r3(bk), r3(bv), band_kc, band_vc, l, rel_row)
        logf = lf[:, :N_HEADS].reshape(bs, ts, N_HEADS)
        fk_cache = _cum_logf(jnp.swapaxes(cache_fox_logf[l], 1, 2), jnp.zeros((bs, N_HEADS, 1), F32))
        fk_new = _cum_logf(jnp.swapaxes(logf, 1, 2), fk_cache[:, :, p_len - 1:])
        yc = _fox_sample(r3(cq), r3(ck), r3(cv), fox_kc, fox_vc, l,
                         jnp.swapaxes(fk_new, 1, 2), fk_cache, fk_new)
        xs = finish(xs, ya.reshape(bs * ts, d_a), yb.reshape(bs * ts, D_ATT), yc.reshape(bs * ts, D_ATT))
        for dst, val in zip(outs[5:], (buf[:, SUBLANES - (CONV_W - 1):], h_last.reshape(bs, d_a),
                                       h4(bk, bs, ts), h4(bv, bs, ts), h4(ck, bs, ts), h4(cv, bs, ts), logf)):
            dst.append(val)

    st = [jnp.stack(o, axis=0) for o in outs]
    return (xp.reshape(bp, s, d), xs.reshape(bs, ts, d),
            st[0], st[1], st[2], st[3], k5.reshape(depth, bp, s, N_HEADS, HEAD_DIM),
            v5.reshape(depth, bp, s, N_HEADS, HEAD_DIM), st[4], *st[5:])
```

```python
import functools

import numpy as np
import jax
import jax.numpy as jnp
from jax import lax
from jax.experimental import pallas as pl
from jax.experimental.pallas import tpu as pltpu

F32 = jnp.float32
BF16 = jnp.bfloat16

HEAD_DIM = 128
CHUNK = 64
N_PREV_CHUNKS = 8
BAND = N_PREV_CHUNKS * CHUNK
REL_CLIP = 256
CONV_W = 4
LRU_C = 8.0
LN_EPS = 1e-5
NEG = -1e30
SCALE = HEAD_DIM ** -0.5
LOG2E = float(np.log2(np.e))
N_HEADS = 4
D_ATT = N_HEADS * HEAD_DIM
LANES = 128
SUBLANES = 8
MIB = 1 << 20
N_SPLIT = 3
REL_ROW = 2 * BAND

PROJ_ROWS = 256
LRU_ROWS = 256
BAND_ROWS = 256
FOX_ROWS = 512
FOX_CACHE_ROWS = 2048
MIX_ROWS = 512
FFN_ROWS = 512
FFN_COLS = 1024


def _params(semantics, vmem_mib):
    return pltpu.CompilerParams(dimension_semantics=semantics, vmem_limit_bytes=vmem_mib * MIB)


def _resident(shape):
    return pl.BlockSpec(shape, lambda *_: (0,) * len(shape), pipeline_mode=pl.Buffered(1))


def _layer_block(layer, shape):
    return pl.BlockSpec((None,) + tuple(shape), lambda *_: (layer,) + (0,) * len(shape),
                        pipeline_mode=pl.Buffered(1))


def _log_sigmoid(z):
    return jnp.minimum(z, 0.0) - jnp.log1p(jnp.exp(-jnp.abs(z)))


def _softplus(z):
    return jnp.maximum(z, 0.0) + jnp.log1p(jnp.exp(-jnp.abs(z)))


def _gelu_tanh(x):
    c = np.sqrt(2.0 / np.pi).astype(np.float32)
    return 0.5 * x * (1.0 + jnp.tanh(c * (x + 0.044715 * (x * x * x))))


def _layer_norm(z, g, b):
    mu = jnp.mean(z, axis=-1, keepdims=True)
    zc = z - mu
    var = jnp.mean(zc * zc, axis=-1, keepdims=True)
    return zc * lax.rsqrt(var + LN_EPS) * g + b


def _rms_norm(y, g):
    return y * lax.rsqrt(jnp.mean(y * y, axis=-1, keepdims=True) + LN_EPS) * g


def _dot_nt(a, b):
    return lax.dot_general(a, b, (((1,), (1,)), ((), ())), preferred_element_type=F32)


def _head(x, h):
    return x[:, h * HEAD_DIM:(h + 1) * HEAD_DIM]


def _head_rows(h, rows):
    return pl.ds(h, rows, stride=N_HEADS)


def _interleaved(a):
    return a.reshape(a.shape[:-3] + (a.shape[-3] * N_HEADS, HEAD_DIM))


def _cumsum_rows(x):
    row = lax.broadcasted_iota(jnp.int32, x.shape, 0)
    shift = 1
    while shift < x.shape[0]:
        x = jnp.where(row >= shift, x + pltpu.roll(x, shift, 0), x)
        shift *= 2
    return x


def _split_bf16(x):
    pieces = []
    for _ in range(N_SPLIT):
        p = x.astype(BF16).astype(F32)
        pieces.append(p)
        x = x - p
    return pieces


def _proj_kernel(*refs, widths, fox_aug, blocks_per_batch):
    if fox_aug:
        (x_ref, w_ref, bf_ref, _, _, a2_ref, bq_ref, bk_ref, bv_ref, lf_ref,
         qa_ref, ka_ref, vt_ref, k5_ref, v5_ref, f_sc) = refs
    else:
        x_ref, w_ref, bf_ref, a2_ref, bq_ref, bk_ref, bv_ref, lf_ref, cq_ref, ck_ref, cv_ref = refs
    xb = x_ref[...].astype(BF16)
    offs = np.concatenate([[0], np.cumsum(widths)])

    def group(g):
        return jnp.dot(xb, w_ref[:, int(offs[g]):int(offs[g + 1])], preferred_element_type=F32)

    a2_ref[...] = group(0)
    bq_ref[...] = group(1)
    bk_ref[...] = group(2)
    bv_ref[...] = group(3)
    logf = _log_sigmoid(group(7) + bf_ref[...])
    lf_ref[...] = logf
    if not fox_aug:
        cq_ref[...] = group(4)
        ck_ref[...] = group(5)
        cv_ref[...] = group(6)
        return

    cq, ck, cv = group(4), group(5), group(6)

    @pl.when(pl.program_id(0) % blocks_per_batch == 0)
    def _():
        f_sc[...] = jnp.zeros(f_sc.shape, F32)

    rows = logf.shape[0]
    cum = _cumsum_rows(logf) + f_sc[...]
    f_sc[...] = cum[rows - 1:rows]
    cum2 = cum * LOG2E
    lane = lax.broadcasted_iota(jnp.int32, (rows, HEAD_DIM), 1)
    for h in range(N_HEADS):
        pieces = _split_bf16(jnp.broadcast_to(cum2[:, h:h + 1], (rows, HEAD_DIM)))
        q_extra = jnp.where(lane < 2 * N_SPLIT, 1.0, 0.0)
        k_extra = jnp.where(lane < N_SPLIT, 1.0, 0.0)
        for n, piece in enumerate(pieces):
            q_extra = jnp.where(lane == n, piece, q_extra)
            k_extra = jnp.where(lane == N_SPLIT + n, -piece, k_extra)
        qa_ref[h, :, :HEAD_DIM] = (_head(cq, h) * (SCALE * LOG2E)).astype(BF16)
        qa_ref[h, :, HEAD_DIM:] = q_extra.astype(BF16)
        ka_ref[h, :, :HEAD_DIM] = _head(ck, h).astype(BF16)
        ka_ref[h, :, HEAD_DIM:] = k_extra.astype(BF16)
        vt_ref[h] = _head(cv, h).T.astype(BF16)
        k5_ref[_head_rows(h, rows), :] = _head(ck, h)
        v5_ref[_head_rows(h, rows), :] = _head(cv, h)


def _proj_sample(x2d, w_all, layer, bf_pad, widths):
    t, d = x2d.shape
    n = w_all.shape[2]
    rows = min(PROJ_ROWS, t)
    order = (0, 1, 2, 3, 7, 4, 5, 6)
    return pl.pallas_call(
        functools.partial(_proj_kernel, widths=widths, fox_aug=False, blocks_per_batch=1),
        grid=(t // rows,),
        in_specs=[pl.BlockSpec((rows, d), lambda i: (i, 0)), _layer_block(layer, (d, n)),
                  _resident((1, widths[-1]))],
        out_specs=[pl.BlockSpec((rows, widths[g]), lambda i: (i, 0)) for g in order],
        out_shape=[jax.ShapeDtypeStruct((t, widths[g]), F32) for g in order],
        compiler_params=_params(("arbitrary",), 48),
        name="proj_sample",
    )(x2d, w_all, bf_pad)


def _proj_prompt(x2d, w_all, layer, bf_pad, widths, k5_all, v5_all, seq):
    t, d = x2d.shape
    n = w_all.shape[2]
    rows = PROJ_ROWS
    bpb = seq // rows
    tok = lambda w, dt: (pl.BlockSpec((rows, w), lambda i: (i, 0)), jax.ShapeDtypeStruct((t, w), dt))
    aug = (pl.BlockSpec((N_HEADS, rows, 2 * HEAD_DIM), lambda i: (0, i, 0)),
           jax.ShapeDtypeStruct((N_HEADS, t, 2 * HEAD_DIM), BF16))
    kv5 = lambda a: (pl.BlockSpec((None, None, rows * N_HEADS, HEAD_DIM),
                                  lambda i: (layer, i // bpb, i % bpb, 0)),
                     jax.ShapeDtypeStruct(a.shape, a.dtype))
    outs = [tok(widths[0], F32), tok(D_ATT, F32), tok(D_ATT, F32), tok(D_ATT, F32), tok(widths[-1], F32),
            aug, aug,
            (pl.BlockSpec((N_HEADS, HEAD_DIM, rows), lambda i: (0, 0, i)),
             jax.ShapeDtypeStruct((N_HEADS, HEAD_DIM, t), BF16)),
            kv5(k5_all), kv5(v5_all)]
    return pl.pallas_call(
        functools.partial(_proj_kernel, widths=widths, fox_aug=True, blocks_per_batch=bpb),
        grid=(t // rows,),
        in_specs=[pl.BlockSpec((rows, d), lambda i: (i, 0)), _layer_block(layer, (d, n)),
                  _resident((1, widths[-1])),
                  pl.BlockSpec(memory_space=pl.ANY), pl.BlockSpec(memory_space=pl.ANY)],
        out_specs=[o[0] for o in outs],
        out_shape=[o[1] for o in outs],
        scratch_shapes=[pltpu.VMEM((1, widths[-1]), F32)],
        input_output_aliases={3: 8, 4: 9},
        compiler_params=_params(("arbitrary",), 52),
        name="proj_prompt",
    )(x2d, w_all, bf_pad, k5_all, v5_all)


def _rglru_kernel(ax_ref, ag_ref, cs_ref, h0_ref, wc_ref, bc_ref, wra_ref, bra_ref, wrx_ref, brx_ref,
                  lam_ref, y_ref, buf_ref, hl_ref, h_sc, tail_sc, *, rows, n_blocks, reset_first):
    t = pl.program_id(1)
    d_a = ax_ref.shape[-1]

    @pl.when(t == 0)
    def _():
        h_sc[...] = h0_ref[0]
        tail_sc[...] = cs_ref[0]

    xa = ax_ref[0]
    ext = jnp.concatenate([tail_sc[...], xa], axis=0)
    xc = bc_ref[...] + wc_ref[CONV_W - 1:CONV_W, :] * xa
    for j in range(1, CONV_W):
        shifted = pltpu.roll(ext, j, 0)[SUBLANES:SUBLANES + rows]
        xc = xc + wc_ref[CONV_W - 1 - j:CONV_W - j, :] * shifted
    tail_sc[...] = xa[rows - SUBLANES:rows]

    xcb = xc.astype(BF16)

    def gate(w_ref, b_ref):
        parts = [jnp.dot(xcb[:, n * HEAD_DIM:(n + 1) * HEAD_DIM], w_ref[n], preferred_element_type=F32)
                 for n in range(n_blocks)]
        return jax.nn.sigmoid(jnp.concatenate(parts, axis=1) + b_ref[...])

    r = gate(wra_ref, bra_ref)
    i_gate = gate(wrx_ref, brx_ref)
    log_a = (-LRU_C) * r * _softplus(-lam_ref[...])
    a = jnp.exp(log_a)
    mult = jnp.exp(0.5 * jnp.log(1.0 - a * a))
    row = lax.broadcasted_iota(jnp.int32, (rows, d_a), 0)
    if reset_first:
        mult = jnp.where(jnp.logical_and(row == 0, t == 0), 1.0, mult)
    u = mult * (i_gate * xc)

    shift = 1
    while shift < rows:
        if shift < SUBLANES:
            a_prev = pltpu.roll(a, shift, 0)
            u_prev = pltpu.roll(u, shift, 0)
            live = row >= shift
            u = jnp.where(live, a * u_prev + u, u)
            a = jnp.where(live, a * a_prev, a)
        else:
            a_prev = jnp.concatenate([jnp.ones((shift, d_a), F32), a[:rows - shift]], axis=0)
            u_prev = jnp.concatenate([jnp.zeros((shift, d_a), F32), u[:rows - shift]], axis=0)
            u = a * u_prev + u
            a = a * a_prev
        shift *= 2
    h = u + a * h_sc[...]
    h_last = h[rows - 1:rows]
    h_sc[...] = h_last

    y_ref[0] = h * _gelu_tanh(ag_ref[0])

    @pl.when(t == pl.num_programs(1) - 1)
    def _():
        buf_ref[0] = xa[rows - SUBLANES:rows]
        hl_ref[0] = h_last


def _rglru(a2, conv_state8, h0, w_conv, b_conv, w_ra, b_ra, w_rx, b_rx, lam, *, reset_first):
    b, t, d2 = a2.shape
    d_a = d2 // 2
    rows = min(LRU_ROWS, t)
    n_blocks = d_a // HEAD_DIM
    row_vec = _resident((1, d_a))
    gate_w = _resident((n_blocks, HEAD_DIM, HEAD_DIM))
    kern = functools.partial(_rglru_kernel, rows=rows, n_blocks=n_blocks, reset_first=reset_first)
    return pl.pallas_call(
        kern,
        grid=(b, t // rows),
        in_specs=[pl.BlockSpec((1, rows, d_a), lambda i, j: (i, j, 0)),
                  pl.BlockSpec((1, rows, d_a), lambda i, j: (i, j, 1)),
                  pl.BlockSpec((1, SUBLANES, d_a), lambda i, j: (i, 0, 0)),
                  pl.BlockSpec((1, 1, d_a), lambda i, j: (i, 0, 0)),
                  _resident((CONV_W, d_a)), row_vec, gate_w, row_vec, gate_w, row_vec, row_vec],
        out_specs=[pl.BlockSpec((1, rows, d_a), lambda i, j: (i, j, 0)),
                   pl.BlockSpec((1, SUBLANES, d_a), lambda i, j: (i, 0, 0)),
                   pl.BlockSpec((1, 1, d_a), lambda i, j: (i, 0, 0))],
        out_shape=[jax.ShapeDtypeStruct((b, t, d_a), F32),
                   jax.ShapeDtypeStruct((b, SUBLANES, d_a), F32),
                   jax.ShapeDtypeStruct((b, 1, d_a), F32)],
        scratch_shapes=[pltpu.VMEM((1, d_a), F32), pltpu.VMEM((SUBLANES, d_a), F32)],
        compiler_params=_params(("arbitrary", "arbitrary"), 48),
        name="rglru_reset" if reset_first else "rglru_carry",
    )(a2, a2, conv_state8, h0, w_conv, b_conv, w_ra, b_ra, w_rx, b_rx, lam)


def _cumsum_kernel(x_ref, off_ref, o_ref):
    x = x_ref[...]
    n = x.shape[1]
    lane = lax.broadcasted_iota(jnp.int32, x.shape, 1)
    shift = 1
    while shift < n:
        prev = pltpu.roll(x, shift, 1)
        x = jnp.where(lane >= shift, x + prev, x)
        shift *= 2
    o_ref[...] = x + off_ref[...]


def _cumsum_lanes(x, off):
    r, n = x.shape
    return pl.pallas_call(
        _cumsum_kernel,
        grid=(1,),
        in_specs=[pl.BlockSpec((r, n), lambda i: (0, 0)), pl.BlockSpec((r, 1), lambda i: (0, 0))],
        out_specs=pl.BlockSpec((r, n), lambda i: (0, 0)),
        out_shape=jax.ShapeDtypeStruct((r, n), F32),
        compiler_params=_params(("arbitrary",), 32),
        name="cumsum_lanes",
    )(x, off)


def _softmax_pv(s, v_h):
    m = jnp.max(s, axis=-1, keepdims=True)
    p = jnp.exp2(s - m)
    den = jnp.sum(p, axis=-1, keepdims=True)
    return jnp.dot(p.astype(BF16), v_h, preferred_element_type=F32) / den


def _flash_step(scores, values, m_sc, l_sc, acc_sc):
    prev = [(m_sc[h], l_sc[h], acc_sc[h]) for h in range(N_HEADS)]
    new = []
    for (m_prev, l_prev, acc_prev), s, v_h in zip(prev, scores, values):
        m_new = jnp.maximum(m_prev, jnp.max(s, axis=-1, keepdims=True))
        alpha = jnp.exp2(m_prev - m_new)
        p = jnp.exp2(s - m_new)
        new.append((m_new, alpha * l_prev + jnp.sum(p, axis=-1, keepdims=True),
                    alpha * acc_prev + jnp.dot(p.astype(BF16), v_h, preferred_element_type=F32)))
    for h, (m_new, l_new, acc_new) in enumerate(new):
        m_sc[h] = m_new
        l_sc[h] = l_new
        acc_sc[h] = acc_new


def _flash_init(m_sc, l_sc, acc_sc):
    m_sc[...] = jnp.full(m_sc.shape, NEG, F32)
    l_sc[...] = jnp.zeros(l_sc.shape, F32)
    acc_sc[...] = jnp.zeros(acc_sc.shape, F32)


def _flash_finish(o_ref, l_sc, acc_sc):
    for h in range(N_HEADS):
        o_ref[0, :, h * HEAD_DIM:(h + 1) * HEAD_DIM] = acc_sc[h] / l_sc[h]


def _flash_scratch(rows):
    return [pltpu.VMEM((N_HEADS, rows, 1), F32), pltpu.VMEM((N_HEADS, rows, 1), F32),
            pltpu.VMEM((N_HEADS, rows, HEAD_DIM), F32)]


def _rel_row(rel_table):
    edge = rel_table[:, 2 * REL_CLIP:]
    row = jnp.concatenate([jnp.broadcast_to(edge, (N_HEADS, BAND - REL_CLIP)), rel_table[:, ::-1],
                           jnp.broadcast_to(edge, (N_HEADS, REL_ROW - BAND - REL_CLIP - 1))], axis=1)
    return row.reshape(N_HEADS, 1, REL_ROW).astype(F32)


def _rel_tile(row_ref, h, rows, cols):
    tile = pltpu.roll(jnp.broadcast_to(row_ref[h], (rows, REL_ROW)), 0, 1, stride=1, stride_axis=0)
    return tile[:, :cols] * LOG2E


def _band_prompt_kernel(q_ref, k0_ref, k1_ref, k2_ref, v0_ref, v1_ref, v2_ref, row_ref, o_ref, bias_sc, *, rows):
    i = pl.program_id(1)

    @pl.when(jnp.logical_and(pl.program_id(0) == 0, i == 0))
    def _():
        qc = lax.broadcasted_iota(jnp.int32, (rows, 3 * rows), 0) // CHUNK
        kc = lax.broadcasted_iota(jnp.int32, (rows, 3 * rows), 1) // CHUNK - (2 * rows) // CHUNK
        visible = jnp.logical_and(kc <= qc, kc >= qc - N_PREV_CHUNKS)
        for h in range(N_HEADS):
            bias_sc[h] = jnp.where(visible, _rel_tile(row_ref, h, rows, 3 * rows), NEG)

    def attend(clip_start):
        q = q_ref[0]
        k = jnp.concatenate([k0_ref[0], k1_ref[0], k2_ref[0]], axis=0).astype(BF16)
        v = jnp.concatenate([v0_ref[0], v1_ref[0], v2_ref[0]], axis=0).astype(BF16)
        for h in range(N_HEADS):
            s = _dot_nt((_head(q, h) * (SCALE * LOG2E)).astype(BF16), _head(k, h)) + bias_sc[h]
            if clip_start:
                col = lax.broadcasted_iota(jnp.int32, (rows, 3 * rows), 1)
                s = jnp.where(col >= (2 - i) * rows, s, NEG)
            o_ref[0, :, h * HEAD_DIM:(h + 1) * HEAD_DIM] = _softmax_pv(s, _head(v, h))

    @pl.when(i < 2)
    def _():
        attend(True)

    @pl.when(i >= 2)
    def _():
        attend(False)


def _band_prompt(q, k, v, rel_row):
    b, s, d = q.shape
    rows = BAND_ROWS
    assert BAND == 2 * rows and s % rows == 0
    cur = pl.BlockSpec((1, rows, d), lambda i, j: (i, j, 0))
    prev1 = pl.BlockSpec((1, rows, d), lambda i, j: (i, jnp.maximum(j - 1, 0), 0))
    prev2 = pl.BlockSpec((1, rows, d), lambda i, j: (i, jnp.maximum(j - 2, 0), 0))
    return pl.pallas_call(
        functools.partial(_band_prompt_kernel, rows=rows),
        grid=(b, s // rows),
        in_specs=[cur, prev2, prev1, cur, prev2, prev1, cur, _resident(rel_row.shape)],
        out_specs=cur,
        out_shape=jax.ShapeDtypeStruct((b, s, d), F32),
        scratch_shapes=[pltpu.VMEM((N_HEADS, rows, 3 * rows), F32)],
        compiler_params=_params(("arbitrary", "arbitrary"), 48),
        name="band_prompt",
    )(q, k, k, k, v, v, v, rel_row)


def _band_sample_kernel(q_ref, kn_ref, vn_ref, kc_ref, vc_ref, row_ref, o_ref, bias_sc, *, t_new, n_cache):
    @pl.when(pl.program_id(0) == 0)
    def _():
        for h in range(N_HEADS):
            bias_sc[h] = _rel_tile(row_ref, h, t_new, n_cache + t_new)

    q = q_ref[0]
    kn = kn_ref[0].astype(BF16)
    vn = vn_ref[0].astype(BF16)
    for h in range(N_HEADS):
        cached = _head_rows(h, n_cache)
        k = jnp.concatenate([kc_ref[cached, :].astype(BF16), _head(kn, h)], axis=0)
        v = jnp.concatenate([vc_ref[cached, :].astype(BF16), _head(vn, h)], axis=0)
        s = _dot_nt((_head(q, h) * (SCALE * LOG2E)).astype(BF16), k) + bias_sc[h]
        o_ref[0, :, h * HEAD_DIM:(h + 1) * HEAD_DIM] = _softmax_pv(s, v)


def _band_sample(q, k_new, v_new, k_cache, v_cache, layer, rel_row):
    b, t, d = q.shape
    n_c = k_cache.shape[2] // N_HEADS
    assert n_c == BAND
    new = pl.BlockSpec((1, t, d), lambda i: (i, 0, 0))
    old = pl.BlockSpec((None, None, n_c * N_HEADS, HEAD_DIM), lambda i: (layer, i, 0, 0))
    return pl.pallas_call(
        functools.partial(_band_sample_kernel, t_new=t, n_cache=n_c),
        grid=(b,),
        in_specs=[new, new, new, old, old, _resident(rel_row.shape)],
        out_specs=new,
        out_shape=jax.ShapeDtypeStruct((b, t, d), F32),
        scratch_shapes=[pltpu.VMEM((N_HEADS, t, n_c + t), F32)],
        compiler_params=_params(("arbitrary",), 32),
        name="band_sample",
    )(q, k_new, v_new, k_cache, v_cache, rel_row)


def _fox_prompt_kernel(qi_ref, kj_ref, qa_ref, ka_ref, vt_ref, o_ref, m_sc, l_sc, acc_sc, *, rows):
    i = qi_ref[pl.program_id(1)]
    j = kj_ref[pl.program_id(1)]

    @pl.when(j == 0)
    def _():
        _flash_init(m_sc, l_sc, acc_sc)

    def attend(diagonal):
        scores = [_dot_nt(ka_ref[h], qa_ref[h]) for h in range(N_HEADS)]
        if diagonal:
            causal = (lax.broadcasted_iota(jnp.int32, (rows, rows), 0)
                      <= lax.broadcasted_iota(jnp.int32, (rows, rows), 1))
            scores = [jnp.where(causal, s, NEG) for s in scores]
        prev = [(m_sc[h], l_sc[h], acc_sc[h]) for h in range(N_HEADS)]
        new = []
        for h, ((m_prev, l_prev, acc_prev), s) in enumerate(zip(prev, scores)):
            m_new = jnp.maximum(m_prev, jnp.max(s, axis=0, keepdims=True))
            alpha = jnp.exp2(m_prev - m_new)
            p = jnp.exp2(s - m_new)
            new.append((m_new, alpha * l_prev + jnp.sum(p, axis=0, keepdims=True),
                        alpha * acc_prev + jnp.dot(vt_ref[h], p.astype(BF16), preferred_element_type=F32)))
        for h, (m_new, l_new, acc_new) in enumerate(new):
            m_sc[h] = m_new
            l_sc[h] = l_new
            acc_sc[h] = acc_new

    @pl.when(j < i)
    def _():
        attend(False)

    @pl.when(j == i)
    def _():
        attend(True)
        for h in range(N_HEADS):
            o_ref[0, :, h * HEAD_DIM:(h + 1) * HEAD_DIM] = (acc_sc[h] / l_sc[h]).T


def _fox_prompt(qa, ka, vt, batch):
    _, t, da = qa.shape
    rows = FOX_ROWS
    n = t // batch // rows
    pairs = [(i, j) for i in range(n) for j in range(i + 1)]
    qi = jnp.asarray([p[0] for p in pairs], jnp.int32)
    kj = jnp.asarray([p[1] for p in pairs], jnp.int32)
    grid_spec = pltpu.PrefetchScalarGridSpec(
        num_scalar_prefetch=2,
        grid=(batch, len(pairs)),
        in_specs=[pl.BlockSpec((N_HEADS, rows, da), lambda b, p, qi, kj: (0, b * n + qi[p], 0)),
                  pl.BlockSpec((N_HEADS, rows, da), lambda b, p, qi, kj: (0, b * n + kj[p], 0)),
                  pl.BlockSpec((N_HEADS, HEAD_DIM, rows), lambda b, p, qi, kj: (0, 0, b * n + kj[p]))],
        out_specs=pl.BlockSpec((1, rows, D_ATT), lambda b, p, qi, kj: (b, qi[p], 0)),
        scratch_shapes=[pltpu.VMEM((N_HEADS, 1, rows), F32), pltpu.VMEM((N_HEADS, 1, rows), F32),
                        pltpu.VMEM((N_HEADS, HEAD_DIM, rows), F32)])
    return pl.pallas_call(
        functools.partial(_fox_prompt_kernel, rows=rows),
        grid_spec=grid_spec,
        out_shape=jax.ShapeDtypeStruct((batch, t // batch, D_ATT), F32),
        compiler_params=_params(("arbitrary", "arbitrary"), 48),
        name="fox_prompt",
    )(qi, kj, qa, ka, vt)


def _fox_sample_kernel(q_ref, kc_ref, vc_ref, kn_ref, vn_ref, fq_ref, fkc_ref, fkn_ref, o_ref,
                       m_sc, l_sc, acc_sc, *, t_new):
    j = pl.program_id(1)

    @pl.when(j == 0)
    def _():
        _flash_init(m_sc, l_sc, acc_sc)

    q = q_ref[0]
    qs = [(_head(q, h) * (SCALE * LOG2E)).astype(BF16) for h in range(N_HEADS)]
    fq = fq_ref[0] * LOG2E

    def bias(fk, h):
        return fq[:, h:h + 1] - fk[h:h + 1, :] * LOG2E

    fkc = fkc_ref[0]
    cached = [_head_rows(h, fkc.shape[1]) for h in range(N_HEADS)]
    _flash_step([_dot_nt(qs[h], kc_ref[cached[h], :].astype(BF16)) + bias(fkc, h) for h in range(N_HEADS)],
                [vc_ref[cached[h], :].astype(BF16) for h in range(N_HEADS)], m_sc, l_sc, acc_sc)

    @pl.when(j == pl.num_programs(1) - 1)
    def _():
        kn = kn_ref[0].astype(BF16)
        vn = vn_ref[0].astype(BF16)
        fkn = fkn_ref[0]
        causal = (lax.broadcasted_iota(jnp.int32, (t_new, t_new), 1)
                  <= lax.broadcasted_iota(jnp.int32, (t_new, t_new), 0))
        _flash_step([jnp.where(causal, _dot_nt(qs[h], _head(kn, h)) + bias(fkn, h), NEG) for h in range(N_HEADS)],
                    [_head(vn, h) for h in range(N_HEADS)], m_sc, l_sc, acc_sc)
        _flash_finish(o_ref, l_sc, acc_sc)


def _fox_sample(q, k_new, v_new, k_cache, v_cache, layer, fq, fk_cache, fk_new):
    b, t, d = q.shape
    p = k_cache.shape[2] // N_HEADS
    rows = min(FOX_CACHE_ROWS, p)
    new = pl.BlockSpec((1, t, d), lambda i, j: (i, 0, 0))
    old = pl.BlockSpec((None, None, rows * N_HEADS, HEAD_DIM), lambda i, j: (layer, i, j, 0))
    return pl.pallas_call(
        functools.partial(_fox_sample_kernel, t_new=t),
        grid=(b, p // rows),
        in_specs=[new, old, old, new, new,
                  pl.BlockSpec((1, t, N_HEADS), lambda i, j: (i, 0, 0)),
                  pl.BlockSpec((1, N_HEADS, rows), lambda i, j: (i, 0, j)),
                  pl.BlockSpec((1, N_HEADS, t), lambda i, j: (i, 0, 0))],
        out_specs=new,
        out_shape=jax.ShapeDtypeStruct((b, t, d), F32),
        scratch_shapes=_flash_scratch(t),
        compiler_params=_params(("arbitrary", "arbitrary"), 48),
        name="fox_sample",
    )(q, k_cache, v_cache, k_new, v_new, fq, fk_cache, fk_new)


def _mix_kernel(x_ref, ya_ref, yb_ref, yc_ref, g_ref, w_ref, lg_ref, lb_ref, o_ref, *, alpha):
    d_a = ya_ref.shape[-1]
    d_b = yb_ref.shape[-1]
    g = g_ref[...]
    y = jnp.concatenate([_rms_norm(ya_ref[...], g[:, :d_a]),
                         _rms_norm(yb_ref[...], g[:, d_a:d_a + d_b]),
                         _rms_norm(yc_ref[...], g[:, d_a + d_b:])], axis=1)
    z = alpha * x_ref[...] + jnp.dot(y.astype(BF16), w_ref[...], preferred_element_type=F32)
    o_ref[...] = _layer_norm(z, lg_ref[...], lb_ref[...])


def _mix(x2d, ya, yb, yc, g_mix, w_out_all, layer, ln_g, ln_b, alpha):
    t, d = x2d.shape
    rows = min(MIX_ROWS, t)
    tok = lambda w: pl.BlockSpec((rows, w), lambda i: (i, 0))
    vec = _resident((1, d))
    return pl.pallas_call(
        functools.partial(_mix_kernel, alpha=alpha),
        grid=(t // rows,),
        in_specs=[tok(d), tok(ya.shape[1]), tok(yb.shape[1]), tok(yc.shape[1]), vec,
                  _layer_block(layer, w_out_all.shape[1:]), vec, vec],
        out_specs=tok(d),
        out_shape=jax.ShapeDtypeStruct((t, d), F32),
        compiler_params=_params(("arbitrary",), 48),
        name="mix",
    )(x2d, ya, yb, yc, g_mix, w_out_all, ln_g, ln_b)


def _ffn_kernel(x_ref, wu_ref, wd_ref, lg_ref, lb_ref, o_ref, xb_sc, acc_sc, *, alpha):
    j = pl.program_id(1)

    @pl.when(j == 0)
    def _():
        xb_sc[...] = x_ref[...].astype(BF16)
        acc_sc[...] = jnp.zeros(acc_sc.shape, F32)

    hid = jnp.maximum(jnp.dot(xb_sc[...], wu_ref[...], preferred_element_type=F32), 0.0)
    acc_sc[...] += jnp.dot((hid * hid).astype(BF16), wd_ref[...], preferred_element_type=F32)

    @pl.when(j == pl.num_programs(1) - 1)
    def _():
        o_ref[...] = _layer_norm(alpha * x_ref[...] + acc_sc[...], lg_ref[...], lb_ref[...])


def _ffn(x2d, w_up_all, w_down_all, layer, ln_g, ln_b, alpha):
    t, d = x2d.shape
    d_ff = w_up_all.shape[2]
    rows, cols = min(FFN_ROWS, t), FFN_COLS
    vec = _resident((1, d))
    return pl.pallas_call(
        functools.partial(_ffn_kernel, alpha=alpha),
        grid=(t // rows, d_ff // cols),
        in_specs=[pl.BlockSpec((rows, d), lambda i, j: (i, 0)),
                  pl.BlockSpec((None, d, cols), lambda i, j: (layer, 0, j)),
                  pl.BlockSpec((None, cols, d), lambda i, j: (layer, j, 0)),
                  vec, vec],
        out_specs=pl.BlockSpec((rows, d), lambda i, j: (i, 0)),
        out_shape=jax.ShapeDtypeStruct((t, d), F32),
        scratch_shapes=[pltpu.VMEM((rows, d), BF16), pltpu.VMEM((rows, d), F32)],
        compiler_params=_params(("arbitrary", "arbitrary"), 56),
        name="ffn",
    )(x2d, w_up_all, w_down_all, ln_g, ln_b)


def _cum_logf(logf_t, off):
    b, h, t = logf_t.shape
    t_pad = -(-t // LANES) * LANES
    x = logf_t.reshape(b * h, t)
    if t_pad != t:
        x = jnp.pad(x, ((0, 0), (0, t_pad - t)))
    return _cumsum_lanes(x, off.reshape(b * h, 1))[:, :t].reshape(b, h, t)


def _row(v):
    return v.reshape(1, -1)


def kernel(x_prompt, x_sample, state_conv, state_lru, cache_band_k, cache_band_v, cache_fox_k, cache_fox_v,
           cache_fox_logf, w_in, w_conv, b_conv, w_rg_a, b_rg_a, w_rg_x, b_rg_x, lru_lambda, rel_bias,
           b_forget, g_mix, w_out, ln1_g, ln1_b, w_up, w_down, ln2_g, ln2_b):
    depth = w_in.shape[0]
    bp, s, d = x_prompt.shape
    bs, ts, _ = x_sample.shape
    d_a = w_conv.shape[-1]
    alpha = float((2 * depth) ** 0.25)
    d_main = 2 * d_a + 6 * D_ATT
    widths = (2 * d_a,) + (D_ATT,) * 6 + (LANES,)
    n_band_p = min(BAND, s)
    p_len = cache_fox_k.shape[2]

    w_in_b = jnp.concatenate([w_in[:, :, :d_main],
                              jnp.pad(w_in[:, :, d_main:], ((0, 0), (0, 0), (0, LANES - N_HEADS)))],
                             axis=2).astype(BF16)
    w_ra_b, w_rx_b = w_rg_a.astype(BF16), w_rg_x.astype(BF16)
    w_out_b, w_up_b, w_down_b = w_out.astype(BF16), w_up.astype(BF16), w_down.astype(BF16)

    xp = x_prompt.reshape(bp * s, d)
    xs = x_sample.reshape(bs * ts, d)
    outs = [[] for _ in range(12)]
    k5 = jnp.zeros((depth, bp, s * N_HEADS, HEAD_DIM), F32)
    v5 = jnp.zeros((depth, bp, s * N_HEADS, HEAD_DIM), F32)
    band_kc, band_vc = _interleaved(cache_band_k), _interleaved(cache_band_v)
    fox_kc, fox_vc = _interleaved(cache_fox_k), _interleaved(cache_fox_v)

    for l in range(depth):
        bf_pad = jnp.pad(b_forget[l], (0, LANES - N_HEADS)).reshape(1, LANES)
        lru_w = (w_conv[l], _row(b_conv[l]), w_ra_b[l], _row(b_rg_a[l]), w_rx_b[l], _row(b_rg_x[l]),
                 _row(lru_lambda[l]))
        rel_row = _rel_row(rel_bias[l])

        def finish(x2d, ya, yb, yc):
            x1 = _mix(x2d, ya, yb, yc, _row(g_mix[l]), w_out_b, l, _row(ln1_g[l]), _row(ln1_b[l]), alpha)
            return _ffn(x1, w_up_b, w_down_b, l, _row(ln2_g[l]), _row(ln2_b[l]), alpha)

        a2, bq, bk, bv, lf, qa, ka, vt, k5, v5 = _proj_prompt(xp, w_in_b, l, bf_pad, widths, k5, v5, s)
        ya, buf, h_last = _rglru(a2.reshape(bp, s, 2 * d_a), jnp.zeros((bp, SUBLANES, d_a), F32),
                                 jnp.zeros((bp, 1, d_a), F32), *lru_w, reset_first=True)
        r3 = lambda z: z.reshape(bp, s, D_ATT)
        yb = _band_prompt(r3(bq), r3(bk), r3(bv), rel_row)
        yc = _fox_prompt(qa, ka, vt, bp)
        xp = finish(xp, ya.reshape(bp * s, d_a), yb.reshape(bp * s, D_ATT), yc.reshape(bp * s, D_ATT))
        h4 = lambda z, b_, t_: z.reshape(b_, t_, N_HEADS, HEAD_DIM)
        newest = lambda z: h4(r3(z)[:, s - n_band_p:], bp, n_band_p)
        for dst, val in zip(outs[:5], (buf[:, SUBLANES - (CONV_W - 1):], h_last.reshape(bp, d_a),
                                       newest(bk), newest(bv),
                                       lf[:, :N_HEADS].reshape(bp, s, N_HEADS))):
            dst.append(val)

        a2, bq, bk, bv, lf, cq, ck, cv = _proj_sample(xs, w_in_b, l, bf_pad, widths)
        conv8 = jnp.pad(state_conv[l], ((0, 0), (SUBLANES - (CONV_W - 1), 0), (0, 0)))
        ya, buf, h_last = _rglru(a2.reshape(bs, ts, 2 * d_a), conv8, state_lru[l].reshape(bs, 1, d_a),
                                 *lru_w, reset_first=False)
        r3 = lambda z: z.reshape(bs, ts, D_ATT)
        yb = _band_sample(r3(bq), r3(bk), r3(bv), band_kc, band_vc, l, rel_row)
        logf = lf[:, :N_HEADS].reshape(bs, ts, N_HEADS)
        fk_cache = _cum_logf(jnp.swapaxes(cache_fox_logf[l], 1, 2), jnp.zeros((bs, N_HEADS, 1), F32))
        fk_new = _cum_logf(jnp.swapaxes(logf, 1, 2), fk_cache[:, :, p_len - 1:])
        yc = _fox_sample(r3(cq), r3(ck), r3(cv), fox_kc, fox_vc, l,
                         jnp.swapaxes(fk_new, 1, 2), fk_cache, fk_new)
        xs = finish(xs, ya.reshape(bs * ts, d_a), yb.reshape(bs * ts, D_ATT), yc.reshape(bs * ts, D_ATT))
        for dst, val in zip(outs[5:], (buf[:, SUBLANES - (CONV_W - 1):], h_last.reshape(bs, d_a),
                                       h4(bk, bs, ts), h4(bv, bs, ts), h4(ck, bs, ts), h4(cv, bs, ts), logf)):
            dst.append(val)

    st = [jnp.stack(o, axis=0) for o in outs]
    return (xp.reshape(bp, s, d), xs.reshape(bs, ts, d),
            st[0], st[1], st[2], st[3], k5.reshape(depth, bp, s, N_HEADS, HEAD_DIM),
            v5.reshape(depth, bp, s, N_HEADS, HEAD_DIM), st[4], *st[5:])
```

```python
import functools

import numpy as np
import jax
import jax.numpy as jnp
from jax import lax
from jax.experimental import pallas as pl
from jax.experimental.pallas import tpu as pltpu

F32 = jnp.float32
BF16 = jnp.bfloat16

HEAD_DIM = 128
CHUNK = 64
N_PREV_CHUNKS = 8
BAND = N_PREV_CHUNKS * CHUNK
REL_CLIP = 256
CONV_W = 4
LRU_C = 8.0
LN_EPS = 1e-5
NEG = -1e30
SCALE = HEAD_DIM ** -0.5
LOG2E = float(np.log2(np.e))
N_HEADS = 4
D_ATT = N_HEADS * HEAD_DIM
LANES = 128
SUBLANES = 8
MIB = 1 << 20
N_SPLIT = 3
REL_ROW = 4 * BAND

PROJ_ROWS = 256
LRU_ROWS = 256
BAND_ROWS = 256
FOX_ROWS = 512
FOX_CACHE_ROWS = 2048
MIX_ROWS = 512
FFN_ROWS = 512
FFN_COLS = 1024


def _params(semantics, vmem_mib):
    return pltpu.CompilerParams(dimension_semantics=semantics, vmem_limit_bytes=vmem_mib * MIB)


def _resident(shape):
    return pl.BlockSpec(shape, lambda *_: (0,) * len(shape), pipeline_mode=pl.Buffered(1))


def _layer_block(layer, shape):
    return pl.BlockSpec((None,) + tuple(shape), lambda *_: (layer,) + (0,) * len(shape),
                        pipeline_mode=pl.Buffered(1))


def _log_sigmoid(z):
    return jnp.minimum(z, 0.0) - jnp.log1p(jnp.exp(-jnp.abs(z)))


def _softplus(z):
    return jnp.maximum(z, 0.0) + jnp.log1p(jnp.exp(-jnp.abs(z)))


def _gelu_tanh(x):
    c = np.sqrt(2.0 / np.pi).astype(np.float32)
    return 0.5 * x * (1.0 + jnp.tanh(c * (x + 0.044715 * (x * x * x))))


def _layer_norm(z, g, b):
    mu = jnp.mean(z, axis=-1, keepdims=True)
    zc = z - mu
    var = jnp.mean(zc * zc, axis=-1, keepdims=True)
    return zc * lax.rsqrt(var + LN_EPS) * g + b


def _rms_norm(y, g):
    return y * lax.rsqrt(jnp.mean(y * y, axis=-1, keepdims=True) + LN_EPS) * g


def _dot_nt(a, b):
    return lax.dot_general(a, b, (((1,), (1,)), ((), ())), preferred_element_type=F32)


def _head(x, h):
    return x[:, h * HEAD_DIM:(h + 1) * HEAD_DIM]


def _head_rows(h, rows):
    return pl.ds(h, rows, stride=N_HEADS)


def _interleaved(a):
    return a.reshape(a.shape[:-3] + (a.shape[-3] * N_HEADS, HEAD_DIM))


def _cumsum_rows(x):
    row = lax.broadcasted_iota(jnp.int32, x.shape, 0)
    shift = 1
    while shift < x.shape[0]:
        x = jnp.where(row >= shift, x + pltpu.roll(x, shift, 0), x)
        shift *= 2
    return x


def _split_bf16(x):
    pieces = []
    for _ in range(N_SPLIT):
        p = x.astype(BF16).astype(F32)
        pieces.append(p)
        x = x - p
    return pieces


def _proj_kernel(*refs, widths, fox_aug, stream_rows, n_aliased):
    x_ref, w_ref, wf_ref, bf_ref = refs[:4]
    if fox_aug:
        (a2_ref, bq_ref, bk_ref, bv_ref, lf_ref,
         qa_ref, ka_ref, vt_ref, k5_ref, v5_ref, f_sc) = refs[4 + n_aliased:]
    else:
        (a2_ref, bq_ref, bk_ref, bv_ref, lf_ref, cq_ref, ck_ref, cv_ref,
         bk5_ref, bv5_ref, ck5_ref, cv5_ref) = refs[4 + n_aliased:]
    xb = x_ref[...].astype(BF16)
    rows = xb.shape[0]
    offs = np.concatenate([[0], np.cumsum(widths)])

    def group(g):
        return jnp.dot(xb, w_ref[:, int(offs[g]):int(offs[g + 1])], preferred_element_type=F32)

    logf =_log_sigmoid(jnp.dot(xb, wf_ref[...], preferred_element_type=F32) + bf_ref[...])
    lf_ref[...] = logf
    cq, ck, cv = group(4), group(5), group(6)
    bk, bv = group(2), group(3)
    bk_ref[...] = bk
    bv_ref[...] = bv
    bq_ref[...] = group(1)
    a2_ref[...] = group(0)
    if not fox_aug:
        cq_ref[...] = cq
        ck_ref[...] = ck
        cv_ref[...] = cv
        for dst_ref, val in ((bk5_ref, bk), (bv5_ref, bv), (ck5_ref, ck), (cv5_ref, cv)):
            for b in range(rows // stream_rows):
                for h in range(N_HEADS):
                    dst_ref[b, _head_rows(h, stream_rows), :] = _head(val[b * stream_rows:(b + 1) * stream_rows], h)
        return

    @pl.when(pl.program_id(0) % (stream_rows // rows) == 0)
    def _():
        f_sc[...] = jnp.zeros(f_sc.shape, F32)

    cum = _cumsum_rows(logf) + f_sc[...]
    f_sc[...] = cum[rows - 1:rows]
    cum2 = cum * LOG2E
    lane = lax.broadcasted_iota(jnp.int32, (rows, HEAD_DIM), 1)
    for h in range(N_HEADS):
        pieces = _split_bf16(jnp.broadcast_to(cum2[:, h:h + 1], (rows, HEAD_DIM)))
        q_extra = jnp.where(lane < 2 * N_SPLIT, 1.0, 0.0)
        k_extra = jnp.where(lane < N_SPLIT, 1.0, 0.0)
        for n, piece in enumerate(pieces):
            q_extra = jnp.where(lane == n, piece, q_extra)
            k_extra = jnp.where(lane == N_SPLIT + n, -piece, k_extra)
        qa_ref[h, :, :HEAD_DIM] = (_head(cq, h) * (SCALE * LOG2E)).astype(BF16)
        qa_ref[h, :, HEAD_DIM:] = q_extra.astype(BF16)
        ka_ref[h, :, :HEAD_DIM] = _head(ck, h).astype(BF16)
        ka_ref[h, :, HEAD_DIM:] = k_extra.astype(BF16)
        vt_ref[h] = _head(cv, h).T.astype(BF16)
        k5_ref[_head_rows(h, rows), :] = _head(ck, h)
        v5_ref[_head_rows(h, rows), :] = _head(cv, h)


def _proj_sample(x2d, w_all, wf_all, layer, bf_pad, widths, kv_all, seq):
    t, d = x2d.shape
    rows = min(PROJ_ROWS, t)
    order = (0, 1, 2, 3, 7, 4, 5, 6)
    final = pl.BlockSpec((None, rows // seq, seq * N_HEADS, HEAD_DIM), lambda i: (layer, i, 0, 0))
    n_in = 4
    return pl.pallas_call(
        functools.partial(_proj_kernel, widths=widths, fox_aug=False, stream_rows=seq, n_aliased=len(kv_all)),
        grid=(t // rows,),
        in_specs=[pl.BlockSpec((rows, d), lambda i: (i, 0)), _layer_block(layer, w_all.shape[1:]),
                  _layer_block(layer, wf_all.shape[1:]), _resident((1, widths[-1]))]
                 + [pl.BlockSpec(memory_space=pl.ANY)] * len(kv_all),
        out_specs=[pl.BlockSpec((rows, widths[g]), lambda i: (i, 0)) for g in order] + [final] * len(kv_all),
        out_shape=[jax.ShapeDtypeStruct((t, widths[g]), F32) for g in order]
                  + [jax.ShapeDtypeStruct(a.shape, a.dtype) for a in kv_all],
        input_output_aliases={n_in + n: len(order) + n for n in range(len(kv_all))},
        compiler_params=_params(("arbitrary",), 48),
        name="proj_sample",
    )(x2d, w_all, wf_all, bf_pad, *kv_all)


def _proj_prompt(x2d, w_all, wf_all, layer, bf_pad, widths, k5_all, v5_all, seq):
    t, d = x2d.shape
    rows = PROJ_ROWS
    bpb = seq // rows
    tok = lambda w, dt: (pl.BlockSpec((rows, w), lambda i: (i, 0)), jax.ShapeDtypeStruct((t, w), dt))
    aug = (pl.BlockSpec((N_HEADS, rows, 2 * HEAD_DIM), lambda i: (0, i, 0)),
           jax.ShapeDtypeStruct((N_HEADS, t, 2 * HEAD_DIM), BF16))
    kv5 = lambda a: (pl.BlockSpec((None, None, rows * N_HEADS, HEAD_DIM),
                                  lambda i: (layer, i // bpb, i % bpb, 0)),
                     jax.ShapeDtypeStruct(a.shape, a.dtype))
    transposed = (pl.BlockSpec((N_HEADS, HEAD_DIM, rows), lambda i: (0, 0, i)),
                  jax.ShapeDtypeStruct((N_HEADS, HEAD_DIM, t), BF16))
    outs = [tok(widths[0], F32), tok(D_ATT, F32), tok(D_ATT, F32), tok(D_ATT, F32), tok(widths[-1], F32),
            aug, aug, transposed, kv5(k5_all), kv5(v5_all)]
    return pl.pallas_call(
        functools.partial(_proj_kernel, widths=widths, fox_aug=True, stream_rows=seq, n_aliased=2),
        grid=(t // rows,),
        in_specs=[pl.BlockSpec((rows, d), lambda i: (i, 0)), _layer_block(layer, w_all.shape[1:]),
                  _layer_block(layer, wf_all.shape[1:]), _resident((1, widths[-1])),
                  pl.BlockSpec(memory_space=pl.ANY), pl.BlockSpec(memory_space=pl.ANY)],
        out_specs=[o[0] for o in outs],
        out_shape=[o[1] for o in outs],
        scratch_shapes=[pltpu.VMEM((1, widths[-1]), F32)],
        input_output_aliases={4: 8, 5: 9},
        compiler_params=_params(("arbitrary",), 52),
        name="proj_prompt",
    )(x2d, w_all, wf_all, bf_pad, k5_all, v5_all)


def _rglru_kernel(ax_ref, ag_ref, cs_ref, h0_ref, wc_ref, bc_ref, wra_ref, bra_ref, wrx_ref, brx_ref,
                  lam_ref, y_ref, buf_ref, hl_ref, h_sc, tail_sc, *, rows, n_blocks, reset_first):
    t = pl.program_id(1)
    d_a = ax_ref.shape[-1]

    @pl.when(t == 0)
    def _():
        h_sc[...] = h0_ref[0]
        tail_sc[...] = cs_ref[0]

    xa = ax_ref[0]
    ext = jnp.concatenate([tail_sc[...], xa], axis=0)
    xc = bc_ref[...] + wc_ref[CONV_W - 1:CONV_W, :] * xa
    for j in range(1, CONV_W):
        shifted = pltpu.roll(ext, j, 0)[SUBLANES:SUBLANES + rows]
        xc = xc + wc_ref[CONV_W - 1 - j:CONV_W - j, :] * shifted
    tail_sc[...] = xa[rows - SUBLANES:rows]

    xcb = xc.astype(BF16)

    def gate(w_ref, b_ref):
        parts = [jnp.dot(xcb[:, n * HEAD_DIM:(n + 1) * HEAD_DIM], w_ref[n], preferred_element_type=F32)
                 for n in range(n_blocks)]
        return jax.nn.sigmoid(jnp.concatenate(parts, axis=1) + b_ref[...])

    r = gate(wra_ref, bra_ref)
    i_gate = gate(wrx_ref, brx_ref)
    log_a = (-LRU_C) * r * _softplus(-lam_ref[...])
    a = jnp.exp(log_a)
    mult = jnp.exp(0.5 * jnp.log(1.0 - a * a))
    row = lax.broadcasted_iota(jnp.int32, (rows, d_a), 0)
    if reset_first:
        mult = jnp.where(jnp.logical_and(row == 0, t == 0), 1.0, mult)
    u = mult * (i_gate * xc)

    n_groups = rows // SUBLANES
    a = a.reshape(n_groups, SUBLANES, d_a)
    u = u.reshape(n_groups, SUBLANES, d_a)
    in_group = lax.broadcasted_iota(jnp.int32, a.shape, 1)
    shift = 1
    while shift < SUBLANES:
        a_prev = pltpu.roll(a, shift, 1)
        u_prev = pltpu.roll(u, shift, 1)
        live = in_group >= shift
        u = jnp.where(live, a * u_prev + u, u)
        a = jnp.where(live, a * a_prev, a)
        shift *= 2
    h_last = h_sc[...]
    groups = []
    for g in range(n_groups):
        h_g = u[g] + a[g] * h_last
        h_last = h_g[SUBLANES - 1:SUBLANES]
        groups.append(h_g)
    h = jnp.concatenate(groups, axis=0)
    h_sc[...] = h_last

    y_ref[0] = h * _gelu_tanh(ag_ref[0])

    @pl.when(t == pl.num_programs(1) - 1)
    def _():
        buf_ref[0] = xa[rows - SUBLANES:rows]
        hl_ref[0] = h_last


def _rglru(a2, conv_state8, h0, w_conv, b_conv, w_ra, b_ra, w_rx, b_rx, lam, *, reset_first):
    b, t, d2 = a2.shape
    d_a = d2 // 2
    rows = min(LRU_ROWS, t)
    n_blocks = d_a // HEAD_DIM
    row_vec = _resident((1, d_a))
    gate_w = _resident((n_blocks, HEAD_DIM, HEAD_DIM))
    kern = functools.partial(_rglru_kernel, rows=rows, n_blocks=n_blocks, reset_first=reset_first)
    return pl.pallas_call(
        kern,
        grid=(b, t // rows),
        in_specs=[pl.BlockSpec((1, rows, d_a), lambda i, j: (i, j, 0)),
                  pl.BlockSpec((1, rows, d_a), lambda i, j: (i, j, 1)),
                  pl.BlockSpec((1, SUBLANES, d_a), lambda i, j: (i, 0, 0)),
                  pl.BlockSpec((1, 1, d_a), lambda i, j: (i, 0, 0)),
                  _resident((CONV_W, d_a)), row_vec, gate_w, row_vec, gate_w, row_vec, row_vec],
        out_specs=[pl.BlockSpec((1, rows, d_a), lambda i, j: (i, j, 0)),
                   pl.BlockSpec((1, SUBLANES, d_a), lambda i, j: (i, 0, 0)),
                   pl.BlockSpec((1, 1, d_a), lambda i, j: (i, 0, 0))],
        out_shape=[jax.ShapeDtypeStruct((b, t, d_a), F32),
                   jax.ShapeDtypeStruct((b, SUBLANES, d_a), F32),
                   jax.ShapeDtypeStruct((b, 1, d_a), F32)],
        scratch_shapes=[pltpu.VMEM((1, d_a), F32), pltpu.VMEM((SUBLANES, d_a), F32)],
        compiler_params=_params(("arbitrary", "arbitrary"), 48),
        name="rglru_reset" if reset_first else "rglru_carry",
    )(a2, a2, conv_state8, h0, w_conv, b_conv, w_ra, b_ra, w_rx, b_rx, lam)


def _cumsum_kernel(x_ref, off_ref, o_ref):
    x = x_ref[...]
    n = x.shape[1]
    lane = lax.broadcasted_iota(jnp.int32, x.shape, 1)
    shift = 1
    while shift < n:
        prev = pltpu.roll(x, shift, 1)
        x = jnp.where(lane >= shift, x + prev, x)
        shift *= 2
    o_ref[...] = x + off_ref[...]


def _cumsum_lanes(x, off):
    r, n = x.shape
    return pl.pallas_call(
        _cumsum_kernel,
        grid=(1,),
        in_specs=[pl.BlockSpec((r, n), lambda i: (0, 0)), pl.BlockSpec((r, 1), lambda i: (0, 0))],
        out_specs=pl.BlockSpec((r, n), lambda i: (0, 0)),
        out_shape=jax.ShapeDtypeStruct((r, n), F32),
        compiler_params=_params(("arbitrary",), 32),
        name="cumsum_lanes",
    )(x, off)


def _softmax_pv(s, v_h):
    m = jnp.max(s, axis=-1, keepdims=True)
    p = jnp.exp2(s - m)
    den = jnp.sum(p, axis=-1, keepdims=True)
    return jnp.dot(p.astype(BF16), v_h, preferred_element_type=F32) / den


def _flash_step(scores, values, m_sc, l_sc, acc_sc):
    prev = [(m_sc[h], l_sc[h], acc_sc[h]) for h in range(N_HEADS)]
    new = []
    for (m_prev, l_prev, acc_prev), s, v_h in zip(prev, scores, values):
        m_new = jnp.maximum(m_prev, jnp.max(s, axis=-1, keepdims=True))
        alpha = jnp.exp2(m_prev - m_new)
        p = jnp.exp2(s - m_new)
        new.append((m_new, alpha * l_prev + jnp.sum(p, axis=-1, keepdims=True),
                    alpha * acc_prev + jnp.dot(p.astype(BF16), v_h, preferred_element_type=F32)))
    for h, (m_new, l_new, acc_new) in enumerate(new):
        m_sc[h] = m_new
        l_sc[h] = l_new
        acc_sc[h] = acc_new


def _flash_init(m_sc, l_sc, acc_sc):
    m_sc[...] = jnp.full(m_sc.shape, NEG, F32)
    l_sc[...] = jnp.zeros(l_sc.shape, F32)
    acc_sc[...] = jnp.zeros(acc_sc.shape, F32)


def _flash_finish(o_ref, l_sc, acc_sc):
    for h in range(N_HEADS):
        o_ref[0, :, h * HEAD_DIM:(h + 1) * HEAD_DIM] = acc_sc[h] / l_sc[h]


def _flash_scratch(rows):
    return [pltpu.VMEM((N_HEADS, rows, 1), F32), pltpu.VMEM((N_HEADS, rows, 1), F32),
            pltpu.VMEM((N_HEADS, rows, HEAD_DIM), F32)]


def _rel_row(rel_table):
    near = rel_table[:, 2 * REL_CLIP:]
    far = rel_table[:, :1]
    n_far = REL_ROW // 2 - BAND - REL_CLIP - 1
    row = jnp.concatenate([jnp.broadcast_to(near, (N_HEADS, BAND - REL_CLIP)), rel_table[:, ::-1],
                           jnp.broadcast_to(far, (N_HEADS, n_far)),
                           jnp.broadcast_to(near, (N_HEADS, REL_ROW // 2))], axis=1)
    return row.reshape(N_HEADS, 1, REL_ROW).astype(F32)


def _rel_tile(row_ref, h, rows, cols):
    tile = pltpu.roll(jnp.broadcast_to(row_ref[h], (rows, REL_ROW)), 0, 1, stride=1, stride_axis=0)
    return tile[:, :cols] * LOG2E


def _band_prompt_kernel(*refs, rows, n_prev):
    n_kv = n_prev + 1
    q_ref, k_refs, v_refs = refs[0], refs[1:1 + n_kv], refs[1 + n_kv:1 + 2 * n_kv]
    row_ref, o_ref, bias_sc = refs[1 + 2 * n_kv:]
    cols = n_kv * rows
    i = pl.program_id(1)

    @pl.when(jnp.logical_and(pl.program_id(0) == 0, i == 0))
    def _():
        qc = lax.broadcasted_iota(jnp.int32, (rows, cols), 0) // CHUNK
        kc = lax.broadcasted_iota(jnp.int32, (rows, cols), 1) // CHUNK - (n_prev * rows) // CHUNK
        visible = jnp.logical_and(kc <= qc, kc >= qc - N_PREV_CHUNKS)
        for h in range(N_HEADS):
            bias_sc[h] = jnp.where(visible, _rel_tile(row_ref, h, rows, cols), NEG)

    def attend(clip_start):
        q = q_ref[0]
        k = jnp.concatenate([r[0] for r in k_refs], axis=0).astype(BF16)
        v = jnp.concatenate([r[0] for r in v_refs], axis=0).astype(BF16)
        for h in range(N_HEADS):
            s = _dot_nt((_head(q, h) * (SCALE * LOG2E)).astype(BF16), _head(k, h)) + bias_sc[h]
            if clip_start:
                col = lax.broadcasted_iota(jnp.int32, (rows, cols), 1)
                s = jnp.where(col >= (n_prev - i) * rows, s, NEG)
            o_ref[0, :, h * HEAD_DIM:(h + 1) * HEAD_DIM] = _softmax_pv(s, _head(v, h))

    @pl.when(i < n_prev)
    def _():
        attend(True)

    @pl.when(i >= n_prev)
    def _():
        attend(False)


def _band_prompt(q, k, v, rel_row):
    b, s, d = q.shape
    rows = BAND_ROWS
    n_prev = BAND // rows
    assert BAND == n_prev * rows and s % rows == 0 and rows + BAND <= REL_ROW // 2
    back = lambda steps: pl.BlockSpec((1, rows, d), lambda i, j: (i, jnp.maximum(j - steps, 0), 0))
    window = [back(steps) for steps in range(n_prev, -1, -1)]
    return pl.pallas_call(
        functools.partial(_band_prompt_kernel, rows=rows, n_prev=n_prev),
        grid=(b, s // rows),
        in_specs=[back(0)] + window + window + [_resident(rel_row.shape)],
        out_specs=back(0),
        out_shape=jax.ShapeDtypeStruct((b, s, d), F32),
        scratch_shapes=[pltpu.VMEM((N_HEADS, rows, (n_prev + 1) * rows), F32)],
        compiler_params=_params(("arbitrary", "arbitrary"), 48),
        name="band_prompt",
    )(q, *[k] * (n_prev + 1), *[v] * (n_prev + 1), rel_row)


def _band_sample_kernel(q_ref, kn_ref, vn_ref, kc_ref, vc_ref, row_ref, o_ref, bias_sc, *, t_new, n_cache):
    @pl.when(pl.program_id(0) == 0)
    def _():
        for h in range(N_HEADS):
            bias_sc[h] = _rel_tile(row_ref, h, t_new, n_cache + t_new)

    q = q_ref[0]
    kn = kn_ref[0].astype(BF16)
    vn = vn_ref[0].astype(BF16)
    for h in range(N_HEADS):
        cached = _head_rows(h, n_cache)
        k = jnp.concatenate([kc_ref[cached, :].astype(BF16), _head(kn, h)], axis=0)
        v = jnp.concatenate([vc_ref[cached, :].astype(BF16), _head(vn, h)], axis=0)
        s = _dot_nt((_head(q, h) * (SCALE * LOG2E)).astype(BF16), k) + bias_sc[h]
        o_ref[0, :, h * HEAD_DIM:(h + 1) * HEAD_DIM] = _softmax_pv(s, v)


def _band_sample(q, k_new, v_new, k_cache, v_cache, layer, rel_row):
    b, t, d = q.shape
    n_c = k_cache.shape[2] // N_HEADS
    assert n_c == BAND
    new = pl.BlockSpec((1, t, d), lambda i: (i, 0, 0))
    old = pl.BlockSpec((None, None, n_c * N_HEADS, HEAD_DIM), lambda i: (layer, i, 0, 0))
    return pl.pallas_call(
        functools.partial(_band_sample_kernel, t_new=t, n_cache=n_c),
        grid=(b,),
        in_specs=[new, new, new, old, old, _resident(rel_row.shape)],
        out_specs=new,
        out_shape=jax.ShapeDtypeStruct((b, t, d), F32),
        scratch_shapes=[pltpu.VMEM((N_HEADS, t, n_c + t), F32)],
        compiler_params=_params(("arbitrary",), 32),
        name="band_sample",
    )(q, k_new, v_new, k_cache, v_cache, rel_row)


def _fox_prompt_kernel(qi_ref, kj_ref, qa_ref, ka_ref, vt_ref, o_ref, m_sc, l_sc, acc_sc, *, rows):
    i = qi_ref[pl.program_id(1)]
    j = kj_ref[pl.program_id(1)]

    @pl.when(j == 0)
    def _():
        _flash_init(m_sc, l_sc, acc_sc)

    def attend(diagonal):
        scores = [_dot_nt(ka_ref[h], qa_ref[h]) for h in range(N_HEADS)]
        if diagonal:
            causal = (lax.broadcasted_iota(jnp.int32, (rows, rows), 0)
                      <= lax.broadcasted_iota(jnp.int32, (rows, rows), 1))
            scores = [jnp.where(causal, s, NEG) for s in scores]
        prev = [(m_sc[h], l_sc[h], acc_sc[h]) for h in range(N_HEADS)]
        new = []
        for h, ((m_prev, l_prev, acc_prev), s) in enumerate(zip(prev, scores)):
            m_new = jnp.maximum(m_prev, jnp.max(s, axis=0, keepdims=True))
            alpha = jnp.exp2(m_prev - m_new)
            p = jnp.exp2(s - m_new)
            new.append((m_new, alpha * l_prev + jnp.sum(p, axis=0, keepdims=True),
                        alpha * acc_prev + jnp.dot(vt_ref[h], p.astype(BF16), preferred_element_type=F32)))
        for h, (m_new, l_new, acc_new) in enumerate(new):
            m_sc[h] = m_new
            l_sc[h] = l_new
            acc_sc[h] = acc_new

    @pl.when(j < i)
    def _():
        attend(False)

    @pl.when(j == i)
    def _():
        attend(True)
        for h in range(N_HEADS):
            o_ref[0, :, h * HEAD_DIM:(h + 1) * HEAD_DIM] = (acc_sc[h] / l_sc[h]).T


def _fox_prompt(qa, ka, vt, batch):
    _, t, da = qa.shape
    rows = FOX_ROWS
    n = t // batch // rows
    pairs = [(i, j) for i in range(n) for j in range(i + 1)]
    qi = jnp.asarray([p[0] for p in pairs], jnp.int32)
    kj = jnp.asarray([p[1] for p in pairs], jnp.int32)
    grid_spec = pltpu.PrefetchScalarGridSpec(
        num_scalar_prefetch=2,
        grid=(batch, len(pairs)),
        in_specs=[pl.BlockSpec((N_HEADS, rows, da), lambda b, p, qi, kj: (0, b * n + qi[p], 0)),
                  pl.BlockSpec((N_HEADS, rows, da), lambda b, p, qi, kj: (0, b * n + kj[p], 0)),
                  pl.BlockSpec((N_HEADS, HEAD_DIM, rows), lambda b, p, qi, kj: (0, 0, b * n + kj[p]))],
        out_specs=pl.BlockSpec((1, rows, D_ATT), lambda b, p, qi, kj: (b, qi[p], 0)),
        scratch_shapes=[pltpu.VMEM((N_HEADS, 1, rows), F32), pltpu.VMEM((N_HEADS, 1, rows), F32),
                        pltpu.VMEM((N_HEADS, HEAD_DIM, rows), F32)])
    return pl.pallas_call(
        functools.partial(_fox_prompt_kernel, rows=rows),
        grid_spec=grid_spec,
        out_shape=jax.ShapeDtypeStruct((batch, t // batch, D_ATT), F32),
        compiler_params=_params(("arbitrary", "arbitrary"), 48),
        name="fox_prompt",
    )(qi, kj, qa, ka, vt)


def _fox_sample_kernel(q_ref, kc_ref, vc_ref, kn_ref, vn_ref, fq_ref, fkc_ref, fkn_ref, o_ref,
                       m_sc, l_sc, acc_sc, *, t_new):
    j = pl.program_id(1)

    @pl.when(j == 0)
    def _():
        _flash_init(m_sc, l_sc, acc_sc)

    q = q_ref[0]
    qs = [(_head(q, h) * (SCALE * LOG2E)).astype(BF16) for h in range(N_HEADS)]
    fq = fq_ref[0] * LOG2E

    def bias(fk, h):
        return fq[:, h:h + 1] - fk[h:h + 1, :] * LOG2E

    fkc = fkc_ref[0]
    cached = [_head_rows(h, fkc.shape[1]) for h in range(N_HEADS)]
    _flash_step([_dot_nt(qs[h], kc_ref[cached[h], :].astype(BF16)) + bias(fkc, h) for h in range(N_HEADS)],
                [vc_ref[cached[h], :].astype(BF16) for h in range(N_HEADS)], m_sc, l_sc, acc_sc)

    @pl.when(j == pl.num_programs(1) - 1)
    def _():
        kn = kn_ref[0].astype(BF16)
        vn = vn_ref[0].astype(BF16)
        fkn = fkn_ref[0]
        causal = (lax.broadcasted_iota(jnp.int32, (t_new, t_new), 1)
                  <= lax.broadcasted_iota(jnp.int32, (t_new, t_new), 0))
        _flash_step([jnp.where(causal, _dot_nt(qs[h], _head(kn, h)) + bias(fkn, h), NEG) for h in range(N_HEADS)],
                    [_head(vn, h) for h in range(N_HEADS)], m_sc, l_sc, acc_sc)
        _flash_finish(o_ref, l_sc, acc_sc)


def _fox_sample(q, k_new, v_new, k_cache, v_cache, layer, fq, fk_cache, fk_new):
    b, t, d = q.shape
    p = k_cache.shape[2] // N_HEADS
    rows = min(FOX_CACHE_ROWS, p)
    new = pl.BlockSpec((1, t, d), lambda i, j: (i, 0, 0))
    old = pl.BlockSpec((None, None, rows * N_HEADS, HEAD_DIM), lambda i, j: (layer, i, j, 0))
    return pl.pallas_call(
        functools.partial(_fox_sample_kernel, t_new=t),
        grid=(b, p // rows),
        in_specs=[new, old, old, new, new,
                  pl.BlockSpec((1, t, N_HEADS), lambda i, j: (i, 0, 0)),
                  pl.BlockSpec((1, N_HEADS, rows), lambda i, j: (i, 0, j)),
                  pl.BlockSpec((1, N_HEADS, t), lambda i, j: (i, 0, 0))],
        out_specs=new,
        out_shape=jax.ShapeDtypeStruct((b, t, d), F32),
        scratch_shapes=_flash_scratch(t),
        compiler_params=_params(("arbitrary", "arbitrary"), 48),
        name="fox_sample",
    )(q, k_cache, v_cache, k_new, v_new, fq, fk_cache, fk_new)


def _mix_kernel(x_ref, ya_ref, yb_ref, yc_ref, g_ref, w_ref, lg_ref, lb_ref, o_ref, *, alpha):
    d_a = ya_ref.shape[-1]
    d_b = yb_ref.shape[-1]
    g = g_ref[...]
    y = jnp.concatenate([_rms_norm(ya_ref[...], g[:, :d_a]),
                         _rms_norm(yb_ref[...], g[:, d_a:d_a + d_b]),
                         _rms_norm(yc_ref[...], g[:, d_a + d_b:])], axis=1)
    z = alpha * x_ref[...] + jnp.dot(y.astype(BF16), w_ref[...], preferred_element_type=F32)
    o_ref[...] = _layer_norm(z, lg_ref[...], lb_ref[...])


def _mix(x2d, ya, yb, yc, g_mix, w_out_all, layer, ln_g, ln_b, alpha):
    t, d = x2d.shape
    rows = min(MIX_ROWS, t)
    tok = lambda w: pl.BlockSpec((rows, w), lambda i: (i, 0))
    vec = _resident((1, d))
    return pl.pallas_call(
        functools.partial(_mix_kernel, alpha=alpha),
        grid=(t // rows,),
        in_specs=[tok(d), tok(ya.shape[1]), tok(yb.shape[1]), tok(yc.shape[1]), vec,
                  _layer_block(layer, w_out_all.shape[1:]), vec, vec],
        out_specs=tok(d),
        out_shape=jax.ShapeDtypeStruct((t, d), F32),
        compiler_params=_params(("arbitrary",), 48),
        name="mix",
    )(x2d, ya, yb, yc, g_mix, w_out_all, ln_g, ln_b)


def _ffn_kernel(x_ref, wu_ref, wd_ref, lg_ref, lb_ref, o_ref, xb_sc, *, alpha):
    j = pl.program_id(1)

    @pl.when(j == 0)
    def _():
        xb_sc[...] = x_ref[...].astype(BF16)
        o_ref[...] = jnp.zeros(o_ref.shape, F32)

    hid = jnp.maximum(jnp.dot(xb_sc[...], wu_ref[...], preferred_element_type=F32), 0.0)
    o_ref[...] += jnp.dot((hid * hid).astype(BF16), wd_ref[...], preferred_element_type=F32)

    @pl.when(j == pl.num_programs(1) - 1)
    def _():
        o_ref[...] = _layer_norm(alpha * x_ref[...] + o_ref[...], lg_ref[...], lb_ref[...])


def _ffn(x2d, w_up_all, w_down_all, layer, ln_g, ln_b, alpha):
    t, d = x2d.shape
    d_ff = w_up_all.shape[2]
    rows, cols = min(FFN_ROWS, t), FFN_COLS
    vec = _resident((1, d))
    return pl.pallas_call(
        functools.partial(_ffn_kernel, alpha=alpha),
        grid=(t // rows, d_ff // cols),
        in_specs=[pl.BlockSpec((rows, d), lambda i, j: (i, 0)),
                  pl.BlockSpec((None, d, cols), lambda i, j: (layer, 0, j)),
                  pl.BlockSpec((None, cols, d), lambda i, j: (layer, j, 0)),
                  vec, vec],
        out_specs=pl.BlockSpec((rows, d), lambda i, j: (i, 0)),
        out_shape=jax.ShapeDtypeStruct((t, d), F32),
        scratch_shapes=[pltpu.VMEM((rows, d), BF16)],
        compiler_params=_params(("arbitrary", "arbitrary"), 56),
        name="ffn",
    )(x2d, w_up_all, w_down_all, ln_g, ln_b)


def _cum_logf(logf_t, off):
    b, h, t = logf_t.shape
    t_pad = -(-t // LANES) * LANES
    x = logf_t.reshape(b * h, t)
    if t_pad != t:
        x = jnp.pad(x, ((0, 0), (0, t_pad - t)))
    return _cumsum_lanes(x, off.reshape(b * h, 1))[:, :t].reshape(b, h, t)


def _row(v):
    return v.reshape(1, -1)


def kernel(x_prompt, x_sample, state_conv, state_lru, cache_band_k, cache_band_v, cache_fox_k, cache_fox_v,
           cache_fox_logf, w_in, w_conv, b_conv, w_rg_a, b_rg_a, w_rg_x, b_rg_x, lru_lambda, rel_bias,
           b_forget, g_mix, w_out, ln1_g, ln1_b, w_up, w_down, ln2_g, ln2_b):
    depth = w_in.shape[0]
    bp, s, d = x_prompt.shape
    bs, ts, _ = x_sample.shape
    d_a = w_conv.shape[-1]
    alpha = float((2 * depth) ** 0.25)
    d_main = 2 * d_a + 6 * D_ATT
    widths = (2 * d_a,) + (D_ATT,) * 6 + (LANES,)
    n_band_p = min(BAND, s)
    p_len = cache_fox_k.shape[2]

    w_in_b = w_in[:, :, :d_main].astype(BF16)
    w_f_b = jnp.pad(w_in[:, :, d_main:], ((0, 0), (0, 0), (0, LANES - N_HEADS))).astype(BF16)
    w_ra_b, w_rx_b = w_rg_a.astype(BF16), w_rg_x.astype(BF16)
    w_out_b, w_up_b, w_down_b = w_out.astype(BF16), w_up.astype(BF16), w_down.astype(BF16)

    xp = x_prompt.reshape(bp * s, d)
    xs = x_sample.reshape(bs * ts, d)
    outs = [[] for _ in range(8)]
    k5 = jnp.zeros((depth, bp, s * N_HEADS, HEAD_DIM), F32)
    v5 = jnp.zeros((depth, bp, s * N_HEADS, HEAD_DIM), F32)
    s_kv = [jnp.zeros((depth, bs, ts * N_HEADS, HEAD_DIM), F32) for _ in range(4)]
    band_kc, band_vc = _interleaved(cache_band_k), _interleaved(cache_band_v)
    fox_kc, fox_vc = _interleaved(cache_fox_k), _interleaved(cache_fox_v)

    for l in range(depth):
        bf_pad = jnp.pad(b_forget[l], (0, LANES - N_HEADS)).reshape(1, LANES)
        lru_w = (w_conv[l], _row(b_conv[l]), w_ra_b[l], _row(b_rg_a[l]), w_rx_b[l], _row(b_rg_x[l]),
                 _row(lru_lambda[l]))
        rel_row = _rel_row(rel_bias[l])

        def finish(x2d, ya, yb, yc):
            x1 = _mix(x2d, ya, yb, yc, _row(g_mix[l]), w_out_b, l, _row(ln1_g[l]), _row(ln1_b[l]), alpha)
            return _ffn(x1, w_up_b, w_down_b, l, _row(ln2_g[l]), _row(ln2_b[l]), alpha)

        a2, bq, bk, bv, lf, qa, ka, vt, k5, v5 = _proj_prompt(xp, w_in_b, w_f_b, l, bf_pad, widths, k5, v5, s)
        ya, buf, h_last = _rglru(a2.reshape(bp, s, 2 * d_a), jnp.zeros((bp, SUBLANES, d_a), F32),
                                 jnp.zeros((bp, 1, d_a), F32), *lru_w, reset_first=True)
        r3 = lambda z: z.reshape(bp, s, D_ATT)
        yb = _band_prompt(r3(bq), r3(bk), r3(bv), rel_row)
        yc = _fox_prompt(qa, ka, vt, bp)
        xp = finish(xp, ya.reshape(bp * s, d_a), yb.reshape(bp * s, D_ATT), yc.reshape(bp * s, D_ATT))
        h4 = lambda z, b_, t_: z.reshape(b_, t_, N_HEADS, HEAD_DIM)
        newest = lambda z: h4(r3(z)[:, s - n_band_p:], bp, n_band_p)
        for dst, val in zip(outs[:5], (buf[:, SUBLANES - (CONV_W - 1):], h_last.reshape(bp, d_a),
                                       newest(bk), newest(bv),
                                       lf[:, :N_HEADS].reshape(bp, s, N_HEADS))):
            dst.append(val)

        a2, bq, bk, bv, lf, cq, ck, cv, *s_kv = _proj_sample(xs, w_in_b, w_f_b, l, bf_pad, widths, s_kv, ts)
        conv8 = jnp.pad(state_conv[l], ((0, 0), (SUBLANES - (CONV_W - 1), 0), (0, 0)))
        ya, buf, h_last = _rglru(a2.reshape(bs, ts, 2 * d_a), conv8, state_lru[l].reshape(bs, 1, d_a),
                                 *lru_w, reset_first=False)
        r3 = lambda z: z.reshape(bs, ts, D_ATT)
        yb = _band_sample(r3(bq), r3(bk), r3(bv), band_kc, band_vc, l, rel_row)
        logf = lf[:, :N_HEADS].reshape(bs, ts, N_HEADS)
        fk_cache = _cum_logf(jnp.swapaxes(cache_fox_logf[l], 1, 2), jnp.zeros((bs, N_HEADS, 1), F32))
        fk_new = _cum_logf(jnp.swapaxes(logf, 1, 2), fk_cache[:, :, p_len - 1:])
        yc = _fox_sample(r3(cq), r3(ck), r3(cv), fox_kc, fox_vc, l,
                         jnp.swapaxes(fk_new, 1, 2), fk_cache, fk_new)
        xs = finish(xs, ya.reshape(bs * ts, d_a), yb.reshape(bs * ts, D_ATT), yc.reshape(bs * ts, D_ATT))
        for dst, val in zip(outs[5:], (buf[:, SUBLANES - (CONV_W - 1):], h_last.reshape(bs, d_a), logf)):
            dst.append(val)

    st = [jnp.stack(o, axis=0) for o in outs]
    heads_p = lambda z: z.reshape(depth, bp, s, N_HEADS, HEAD_DIM)
    heads_s = lambda z: z.reshape(depth, bs, ts, N_HEADS, HEAD_DIM)
    return (xp.reshape(bp, s, d), xs.reshape(bs, ts, d),
            st[0], st[1], st[2], st[3], heads_p(k5), heads_p(v5), st[4],
            st[5], st[6], *[heads_s(z) for z in s_kv], st[7])
```

```python
import functools

import numpy as np
import jax
import jax.numpy as jnp
from jax import lax
from jax.experimental import pallas as pl
from jax.experimental.pallas import tpu as pltpu

F32 = jnp.float32
BF16 = jnp.bfloat16

HEAD_DIM = 128
CHUNK = 64
N_PREV_CHUNKS = 8
BAND = N_PREV_CHUNKS * CHUNK
REL_CLIP = 256
CONV_W = 4
LRU_C = 8.0
LN_EPS = 1e-5
NEG = -1e30
SCALE = HEAD_DIM ** -0.5
LOG2E = float(np.log2(np.e))
N_HEADS = 4
D_ATT = N_HEADS * HEAD_DIM
LANES = 128
SUBLANES = 8
MIB = 1 << 20
N_SPLIT = 3
REL_ROW = 4 * BAND

PROJ_ROWS = 256
LRU_ROWS = 256
BAND_ROWS = 256
FOX_ROWS = 512
FOX_CACHE_ROWS = 2048
MIX_ROWS = 512
FFN_ROWS = 512
FFN_COLS = 1024
FFN_EDGE_PARTS = 4


def _params(semantics, vmem_mib):
    return pltpu.CompilerParams(dimension_semantics=semantics, vmem_limit_bytes=vmem_mib * MIB)


def _resident(shape):
    return pl.BlockSpec(shape, lambda *_: (0,) * len(shape), pipeline_mode=pl.Buffered(1))


def _layer_block(layer, shape):
    return pl.BlockSpec((None,) + tuple(shape), lambda *_: (layer,) + (0,) * len(shape),
                        pipeline_mode=pl.Buffered(1))


def _log_sigmoid(z):
    return jnp.minimum(z, 0.0) - jnp.log1p(jnp.exp(-jnp.abs(z)))


def _softplus(z):
    return jnp.maximum(z, 0.0) + jnp.log1p(jnp.exp(-jnp.abs(z)))


def _gelu_tanh(x):
    c = np.sqrt(2.0 / np.pi).astype(np.float32)
    return 0.5 * x * (1.0 + jnp.tanh(c * (x + 0.044715 * (x * x * x))))


def _layer_norm(z, g, b):
    mu = jnp.mean(z, axis=-1, keepdims=True)
    zc = z - mu
    var = jnp.mean(zc * zc, axis=-1, keepdims=True)
    return zc * lax.rsqrt(var + LN_EPS) * g + b


def _rms_norm(y, g):
    return y * lax.rsqrt(jnp.mean(y * y, axis=-1, keepdims=True) + LN_EPS) * g


def _dot_nt(a, b):
    return lax.dot_general(a, b, (((1,), (1,)), ((), ())), preferred_element_type=F32)


def _head(x, h):
    return x[:, h * HEAD_DIM:(h + 1) * HEAD_DIM]


def _head_rows(h, rows):
    return pl.ds(h, rows, stride=N_HEADS)


def _interleaved(a):
    return a.reshape(a.shape[:-3] + (a.shape[-3] * N_HEADS, HEAD_DIM))


def _cumsum_rows(x):
    row = lax.broadcasted_iota(jnp.int32, x.shape, 0)
    shift = 1
    while shift < x.shape[0]:
        x = jnp.where(row >= shift, x + pltpu.roll(x, shift, 0), x)
        shift *= 2
    return x


def _split_bf16(x):
    pieces = []
    for _ in range(N_SPLIT):
        p = x.astype(BF16).astype(F32)
        pieces.append(p)
        x = x - p
    return pieces


def _proj_kernel(*refs, widths, fox_aug, stream_rows, n_aliased):
    x_ref, w_ref, wf_ref, bf_ref = refs[:4]
    if fox_aug:
        (a2_ref, bq_ref, bk_ref, bv_ref, lf_ref,
         qa_ref, ka_ref, vt_ref, k5_ref, v5_ref, f_sc) = refs[4 + n_aliased:]
    else:
        (a2_ref, bq_ref, bk_ref, bv_ref, lf_ref, cq_ref, ck_ref, cv_ref,
         bk5_ref, bv5_ref, ck5_ref, cv5_ref) = refs[4 + n_aliased:]
    xb = x_ref[...].astype(BF16)
    rows = xb.shape[0]
    offs = np.concatenate([[0], np.cumsum(widths)])

    def group(g):
        return _dot_nt(xb, w_ref[int(offs[g]):int(offs[g + 1]), :])

    logf = _log_sigmoid(_dot_nt(xb, wf_ref[...]) + bf_ref[...])
    lf_ref[...] = logf
    cq, ck, cv = group(4), group(5), group(6)
    bk, bv = group(2), group(3)
    bk_ref[...] = bk
    bv_ref[...] = bv
    bq_ref[...] = group(1)
    a2_ref[...] = group(0)
    if not fox_aug:
        cq_ref[...] = cq
        ck_ref[...] = ck
        cv_ref[...] = cv
        for dst_ref, val in ((bk5_ref, bk), (bv5_ref, bv), (ck5_ref, ck), (cv5_ref, cv)):
            for b in range(rows // stream_rows):
                for h in range(N_HEADS):
                    dst_ref[b, _head_rows(h, stream_rows), :] = _head(val[b * stream_rows:(b + 1) * stream_rows], h)
        return

    @pl.when(pl.program_id(0) % (stream_rows // rows) == 0)
    def _():
        f_sc[...] = jnp.zeros(f_sc.shape, F32)

    cum = _cumsum_rows(logf) + f_sc[...]
    f_sc[...] = cum[rows - 1:rows]
    cum2 = cum * LOG2E
    lane = lax.broadcasted_iota(jnp.int32, (rows, HEAD_DIM), 1)
    for h in range(N_HEADS):
        pieces = _split_bf16(jnp.broadcast_to(cum2[:, h:h + 1], (rows, HEAD_DIM)))
        q_extra = jnp.where(lane < 2 * N_SPLIT, 1.0, 0.0)
        k_extra = jnp.where(lane < N_SPLIT, 1.0, 0.0)
        for n, piece in enumerate(pieces):
            q_extra = jnp.where(lane == n, piece, q_extra)
            k_extra = jnp.where(lane == N_SPLIT + n, -piece, k_extra)
        qa_ref[h, :, :HEAD_DIM] = (_head(cq, h) * (SCALE * LOG2E)).astype(BF16)
        qa_ref[h, :, HEAD_DIM:] = q_extra.astype(BF16)
        ka_ref[h, :, :HEAD_DIM] = _head(ck, h).astype(BF16)
        ka_ref[h, :, HEAD_DIM:] = k_extra.astype(BF16)
        vt_ref[h] = _head(cv, h).T.astype(BF16)
        k5_ref[_head_rows(h, rows), :] = _head(ck, h)
        v5_ref[_head_rows(h, rows), :] = _head(cv, h)


def _proj_sample(x2d, w_all, wf_all, layer, bf_pad, widths, kv_all, seq):
    t, d = x2d.shape
    rows = min(PROJ_ROWS, t)
    order = (0, 1, 2, 3, 7, 4, 5, 6)
    final = pl.BlockSpec((None, rows // seq, seq * N_HEADS, HEAD_DIM), lambda i: (layer, i, 0, 0))
    n_in = 4
    return pl.pallas_call(
        functools.partial(_proj_kernel, widths=widths, fox_aug=False, stream_rows=seq, n_aliased=len(kv_all)),
        grid=(t // rows,),
        in_specs=[pl.BlockSpec((rows, d), lambda i: (i, 0)), pl.BlockSpec((sum(widths[:-1]), d), lambda i: (0, layer), pipeline_mode=pl.Buffered(1)),
                  pl.BlockSpec((widths[-1], d), lambda i: (0, layer), pipeline_mode=pl.Buffered(1)), _resident((1, widths[-1]))]
                 + [pl.BlockSpec(memory_space=pl.ANY)] * len(kv_all),
        out_specs=[pl.BlockSpec((rows, widths[g]), lambda i: (i, 0)) for g in order] + [final] * len(kv_all),
        out_shape=[jax.ShapeDtypeStruct((t, widths[g]), F32) for g in order]
                  + [jax.ShapeDtypeStruct(a.shape, a.dtype) for a in kv_all],
        input_output_aliases={n_in + n: len(order) + n for n in range(len(kv_all))},
        compiler_params=_params(("arbitrary",), 48),
        name="proj_sample",
    )(x2d, w_all, wf_all, bf_pad, *kv_all)


def _proj_prompt(x2d, w_all, wf_all, layer, bf_pad, widths, k5_all, v5_all, seq):
    t, d = x2d.shape
    rows = PROJ_ROWS
    bpb = seq // rows
    tok = lambda w, dt: (pl.BlockSpec((rows, w), lambda i: (i, 0)), jax.ShapeDtypeStruct((t, w), dt))
    aug = (pl.BlockSpec((N_HEADS, rows, 2 * HEAD_DIM), lambda i: (0, i, 0)),
           jax.ShapeDtypeStruct((N_HEADS, t, 2 * HEAD_DIM), BF16))
    kv5 = lambda a: (pl.BlockSpec((None, None, rows * N_HEADS, HEAD_DIM),
                                  lambda i: (layer, i // bpb, i % bpb, 0)),
                     jax.ShapeDtypeStruct(a.shape, a.dtype))
    transposed = (pl.BlockSpec((N_HEADS, HEAD_DIM, rows), lambda i: (0, 0, i)),
                  jax.ShapeDtypeStruct((N_HEADS, HEAD_DIM, t), BF16))
    outs = [tok(widths[0], F32), tok(D_ATT, F32), tok(D_ATT, F32), tok(D_ATT, F32), tok(widths[-1], F32),
            aug, aug, transposed, kv5(k5_all), kv5(v5_all)]
    return pl.pallas_call(
        functools.partial(_proj_kernel, widths=widths, fox_aug=True, stream_rows=seq, n_aliased=2),
        grid=(t // rows,),
        in_specs=[pl.BlockSpec((rows, d), lambda i: (i, 0)), pl.BlockSpec((sum(widths[:-1]), d), lambda i: (0, layer), pipeline_mode=pl.Buffered(1)),
                  pl.BlockSpec((widths[-1], d), lambda i: (0, layer), pipeline_mode=pl.Buffered(1)), _resident((1, widths[-1])),
                  pl.BlockSpec(memory_space=pl.ANY), pl.BlockSpec(memory_space=pl.ANY)],
        out_specs=[o[0] for o in outs],
        out_shape=[o[1] for o in outs],
        scratch_shapes=[pltpu.VMEM((1, widths[-1]), F32)],
        input_output_aliases={4: 8, 5: 9},
        compiler_params=_params(("arbitrary",), 52),
        name="proj_prompt",
    )(x2d, w_all, wf_all, bf_pad, k5_all, v5_all)


def _rglru_kernel(ax_ref, ag_ref, cs_ref, h0_ref, wc_ref, bc_ref, wra_ref, bra_ref, wrx_ref, brx_ref,
                  lam_ref, y_ref, buf_ref, hl_ref, h_sc, tail_sc, *, rows, n_blocks, reset_first):
    t = pl.program_id(1)
    d_a = ax_ref.shape[-1]

    @pl.when(t == 0)
    def _():
        h_sc[...] = h0_ref[0]
        tail_sc[...] = cs_ref[0]

    xa = ax_ref[0]
    ext = jnp.concatenate([tail_sc[...], xa], axis=0)
    xc = bc_ref[...] + wc_ref[CONV_W - 1:CONV_W, :] * xa
    for j in range(1, CONV_W):
        shifted = pltpu.roll(ext, j, 0)[SUBLANES:SUBLANES + rows]
        xc = xc + wc_ref[CONV_W - 1 - j:CONV_W - j, :] * shifted
    tail_sc[...] = xa[rows - SUBLANES:rows]

    xcb = xc.astype(BF16)

    def gate(w_ref, b_ref):
        parts = [jnp.dot(xcb[:, n * HEAD_DIM:(n + 1) * HEAD_DIM], w_ref[n], preferred_element_type=F32)
                 for n in range(n_blocks)]
        return jax.nn.sigmoid(jnp.concatenate(parts, axis=1) + b_ref[...])

    r = gate(wra_ref, bra_ref)
    i_gate = gate(wrx_ref, brx_ref)
    log_a = (-LRU_C) * r * _softplus(-lam_ref[...])
    a = jnp.exp(log_a)
    mult = jnp.exp(0.5 * jnp.log(1.0 - a * a))
    row = lax.broadcasted_iota(jnp.int32, (rows, d_a), 0)
    if reset_first:
        mult = jnp.where(jnp.logical_and(row == 0, t == 0), 1.0, mult)
    u = mult * (i_gate * xc)

    n_groups = rows // SUBLANES
    a = a.reshape(n_groups, SUBLANES, d_a)
    u = u.reshape(n_groups, SUBLANES, d_a)
    in_group = lax.broadcasted_iota(jnp.int32, a.shape, 1)
    shift = 1
    while shift < SUBLANES:
        a_prev = pltpu.roll(a, shift, 1)
        u_prev = pltpu.roll(u, shift, 1)
        live = in_group >= shift
        u = jnp.where(live, a * u_prev + u, u)
        a = jnp.where(live, a * a_prev, a)
        shift *= 2
    h_last = h_sc[...]
    groups = []
    for g in range(n_groups):
        h_g = u[g] + a[g] * h_last
        h_last = h_g[SUBLANES - 1:SUBLANES]
        groups.append(h_g)
    h = jnp.concatenate(groups, axis=0)
    h_sc[...] = h_last

    y_ref[0] = h * _gelu_tanh(ag_ref[0])

    @pl.when(t == pl.num_programs(1) - 1)
    def _():
        buf_ref[0] = xa[rows - SUBLANES:rows]
        hl_ref[0] = h_last


def _rglru(a2, conv_state8, h0, w_conv, b_conv, w_ra, b_ra, w_rx, b_rx, lam, *, reset_first):
    b, t, d2 = a2.shape
    d_a = d2 // 2
    rows = min(LRU_ROWS, t)
    n_blocks = d_a // HEAD_DIM
    row_vec = _resident((1, d_a))
    gate_w = _resident((n_blocks, HEAD_DIM, HEAD_DIM))
    kern = functools.partial(_rglru_kernel, rows=rows, n_blocks=n_blocks, reset_first=reset_first)
    return pl.pallas_call(
        kern,
        grid=(b, t // rows),
        in_specs=[pl.BlockSpec((1, rows, d_a), lambda i, j: (i, j, 0)),
                  pl.BlockSpec((1, rows, d_a), lambda i, j: (i, j, 1)),
                  pl.BlockSpec((1, SUBLANES, d_a), lambda i, j: (i, 0, 0)),
                  pl.BlockSpec((1, 1, d_a), lambda i, j: (i, 0, 0)),
                  _resident((CONV_W, d_a)), row_vec, gate_w, row_vec, gate_w, row_vec, row_vec],
        out_specs=[pl.BlockSpec((1, rows, d_a), lambda i, j: (i, j, 0)),
                   pl.BlockSpec((1, SUBLANES, d_a), lambda i, j: (i, 0, 0)),
                   pl.BlockSpec((1, 1, d_a), lambda i, j: (i, 0, 0))],
        out_shape=[jax.ShapeDtypeStruct((b, t, d_a), F32),
                   jax.ShapeDtypeStruct((b, SUBLANES, d_a), F32),
                   jax.ShapeDtypeStruct((b, 1, d_a), F32)],
        scratch_shapes=[pltpu.VMEM((1, d_a), F32), pltpu.VMEM((SUBLANES, d_a), F32)],
        compiler_params=_params(("arbitrary", "arbitrary"), 48),
        name="rglru_reset" if reset_first else "rglru_carry",
    )(a2, a2, conv_state8, h0, w_conv, b_conv, w_ra, b_ra, w_rx, b_rx, lam)


def _cumsum_kernel(x_ref, off_ref, o_ref):
    x = x_ref[...]
    n = x.shape[1]
    lane = lax.broadcasted_iota(jnp.int32, x.shape, 1)
    shift = 1
    while shift < n:
        prev = pltpu.roll(x, shift, 1)
        x = jnp.where(lane >= shift, x + prev, x)
        shift *= 2
    o_ref[...] = x + off_ref[...]


def _cumsum_lanes(x, off):
    r, n = x.shape
    return pl.pallas_call(
        _cumsum_kernel,
        grid=(1,),
        in_specs=[pl.BlockSpec((r, n), lambda i: (0, 0)), pl.BlockSpec((r, 1), lambda i: (0, 0))],
        out_specs=pl.BlockSpec((r, n), lambda i: (0, 0)),
        out_shape=jax.ShapeDtypeStruct((r, n), F32),
        compiler_params=_params(("arbitrary",), 32),
        name="cumsum_lanes",
    )(x, off)


def _softmax_pv(s, v_h):
    m = jnp.max(s, axis=-1, keepdims=True)
    p = jnp.exp2(s - m)
    den = jnp.sum(p, axis=-1, keepdims=True)
    return jnp.dot(p.astype(BF16), v_h, preferred_element_type=F32) / den


def _flash_step(scores, values, m_sc, l_sc, acc_sc):
    prev = [(m_sc[h], l_sc[h], acc_sc[h]) for h in range(N_HEADS)]
    new = []
    for (m_prev, l_prev, acc_prev), s, v_h in zip(prev, scores, values):
        m_new = jnp.maximum(m_prev, jnp.max(s, axis=-1, keepdims=True))
        alpha = jnp.exp2(m_prev - m_new)
        p = jnp.exp2(s - m_new)
        new.append((m_new, alpha * l_prev + jnp.sum(p, axis=-1, keepdims=True),
                    alpha * acc_prev + jnp.dot(p.astype(BF16), v_h, preferred_element_type=F32)))
    for h, (m_new, l_new, acc_new) in enumerate(new):
        m_sc[h] = m_new
        l_sc[h] = l_new
        acc_sc[h] = acc_new


def _flash_init(m_sc, l_sc, acc_sc):
    m_sc[...] = jnp.full(m_sc.shape, NEG, F32)
    l_sc[...] = jnp.zeros(l_sc.shape, F32)
    acc_sc[...] = jnp.zeros(acc_sc.shape, F32)


def _flash_finish(o_ref, l_sc, acc_sc):
    for h in range(N_HEADS):
        o_ref[0, :, h * HEAD_DIM:(h + 1) * HEAD_DIM] = acc_sc[h] / l_sc[h]


def _flash_scratch(rows):
    return [pltpu.VMEM((N_HEADS, rows, 1), F32), pltpu.VMEM((N_HEADS, rows, 1), F32),
            pltpu.VMEM((N_HEADS, rows, HEAD_DIM), F32)]


def _rel_row(rel_table):
    near = rel_table[:, 2 * REL_CLIP:]
    far = rel_table[:, :1]
    n_far = REL_ROW // 2 - BAND - REL_CLIP - 1
    row = jnp.concatenate([jnp.broadcast_to(near, (N_HEADS, BAND - REL_CLIP)), rel_table[:, ::-1],
                           jnp.broadcast_to(far, (N_HEADS, n_far)),
                           jnp.broadcast_to(near, (N_HEADS, REL_ROW // 2))], axis=1)
    return row.reshape(N_HEADS, 1, REL_ROW).astype(F32)


def _rel_tile(row_ref, h, rows, cols):
    tile = pltpu.roll(jnp.broadcast_to(row_ref[h], (rows, REL_ROW)), 0, 1, stride=1, stride_axis=0)
    return tile[:, :cols] * LOG2E


def _band_prompt_kernel(*refs, rows, n_prev):
    n_kv = n_prev + 1
    q_ref, k_refs, v_refs = refs[0], refs[1:1 + n_kv], refs[1 + n_kv:1 + 2 * n_kv]
    row_ref, o_ref, bias_sc = refs[1 + 2 * n_kv:]
    cols = n_kv * rows
    i = pl.program_id(1)

    @pl.when(jnp.logical_and(pl.program_id(0) == 0, i == 0))
    def _():
        qc = lax.broadcasted_iota(jnp.int32, (rows, cols), 0) // CHUNK
        kc = lax.broadcasted_iota(jnp.int32, (rows, cols), 1) // CHUNK - (n_prev * rows) // CHUNK
        visible = jnp.logical_and(kc <= qc, kc >= qc - N_PREV_CHUNKS)
        for h in range(N_HEADS):
            bias_sc[h] = jnp.where(visible, _rel_tile(row_ref, h, rows, cols), NEG)

    def attend(clip_start):
        q = q_ref[0]
        k = jnp.concatenate([r[0] for r in k_refs], axis=0).astype(BF16)
        v = jnp.concatenate([r[0] for r in v_refs], axis=0).astype(BF16)
        for h in range(N_HEADS):
            s = _dot_nt((_head(q, h) * (SCALE * LOG2E)).astype(BF16), _head(k, h)) + bias_sc[h]
            if clip_start:
                col = lax.broadcasted_iota(jnp.int32, (rows, cols), 1)
                s = jnp.where(col >= (n_prev - i) * rows, s, NEG)
            o_ref[0, :, h * HEAD_DIM:(h + 1) * HEAD_DIM] = _softmax_pv(s, _head(v, h))

    @pl.when(i < n_prev)
    def _():
        attend(True)

    @pl.when(i >= n_prev)
    def _():
        attend(False)


def _band_prompt(q, k, v, rel_row):
    b, s, d = q.shape
    rows = BAND_ROWS
    n_prev = BAND // rows
    assert BAND == n_prev * rows and s % rows == 0 and rows + BAND <= REL_ROW // 2
    back = lambda steps: pl.BlockSpec((1, rows, d), lambda i, j: (i, jnp.maximum(j - steps, 0), 0))
    window = [back(steps) for steps in range(n_prev, -1, -1)]
    return pl.pallas_call(
        functools.partial(_band_prompt_kernel, rows=rows, n_prev=n_prev),
        grid=(b, s // rows),
        in_specs=[back(0)] + window + window + [_resident(rel_row.shape)],
        out_specs=back(0),
        out_shape=jax.ShapeDtypeStruct((b, s, d), F32),
        scratch_shapes=[pltpu.VMEM((N_HEADS, rows, (n_prev + 1) * rows), F32)],
        compiler_params=_params(("arbitrary", "arbitrary"), 48),
        name="band_prompt",
    )(q, *[k] * (n_prev + 1), *[v] * (n_prev + 1), rel_row)


def _band_sample_kernel(q_ref, kn_ref, vn_ref, kc_ref, vc_ref, row_ref, o_ref, bias_sc, *, t_new, n_cache):
    @pl.when(pl.program_id(0) == 0)
    def _():
        for h in range(N_HEADS):
            bias_sc[h] = _rel_tile(row_ref, h, t_new, n_cache + t_new)

    q = q_ref[0]
    kn = kn_ref[0].astype(BF16)
    vn = vn_ref[0].astype(BF16)
    for h in range(N_HEADS):
        cached = _head_rows(h, n_cache)
        k = jnp.concatenate([kc_ref[cached, :].astype(BF16), _head(kn, h)], axis=0)
        v = jnp.concatenate([vc_ref[cached, :].astype(BF16), _head(vn, h)], axis=0)
        s = _dot_nt((_head(q, h) * (SCALE * LOG2E)).astype(BF16), k) + bias_sc[h]
        o_ref[0, :, h * HEAD_DIM:(h + 1) * HEAD_DIM] = _softmax_pv(s, v)


def _band_sample(q, k_new, v_new, k_cache, v_cache, layer, rel_row):
    b, t, d = q.shape
    n_c = k_cache.shape[2] // N_HEADS
    assert n_c == BAND
    new = pl.BlockSpec((1, t, d), lambda i: (i, 0, 0))
    old = pl.BlockSpec((None, None, n_c * N_HEADS, HEAD_DIM), lambda i: (layer, i, 0, 0))
    return pl.pallas_call(
        functools.partial(_band_sample_kernel, t_new=t, n_cache=n_c),
        grid=(b,),
        in_specs=[new, new, new, old, old, _resident(rel_row.shape)],
        out_specs=new,
        out_shape=jax.ShapeDtypeStruct((b, t, d), F32),
        scratch_shapes=[pltpu.VMEM((N_HEADS, t, n_c + t), F32)],
        compiler_params=_params(("arbitrary",), 32),
        name="band_sample",
    )(q, k_new, v_new, k_cache, v_cache, rel_row)


def _fox_prompt_kernel(qi_ref, kj_ref, qa_ref, ka_ref, vt_ref, o_ref, m_sc, l_sc, acc_sc, *, rows):
    i = qi_ref[pl.program_id(1)]
    j = kj_ref[pl.program_id(1)]

    @pl.when(j == 0)
    def _():
        _flash_init(m_sc, l_sc, acc_sc)

    def attend(diagonal):
        scores = [_dot_nt(ka_ref[h], qa_ref[h]) for h in range(N_HEADS)]
        if diagonal:
            causal = (lax.broadcasted_iota(jnp.int32, (rows, rows), 0)
                      <= lax.broadcasted_iota(jnp.int32, (rows, rows), 1))
            scores = [jnp.where(causal, s, NEG) for s in scores]
        prev = [(m_sc[h], l_sc[h], acc_sc[h]) for h in range(N_HEADS)]
        new = []
        for h, ((m_prev, l_prev, acc_prev), s) in enumerate(zip(prev, scores)):
            m_new = jnp.maximum(m_prev, jnp.max(s, axis=0, keepdims=True))
            alpha = jnp.exp2(m_prev - m_new)
            p = jnp.exp2(s - m_new)
            new.append((m_new, alpha * l_prev + jnp.sum(p, axis=0, keepdims=True),
                        alpha * acc_prev + jnp.dot(vt_ref[h], p.astype(BF16), preferred_element_type=F32)))
        for h, (m_new, l_new, acc_new) in enumerate(new):
            m_sc[h] = m_new
            l_sc[h] = l_new
            acc_sc[h] = acc_new

    @pl.when(j < i)
    def _():
        attend(False)

    @pl.when(j == i)
    def _():
        attend(True)
        for h in range(N_HEADS):
            o_ref[0, :, h * HEAD_DIM:(h + 1) * HEAD_DIM] = (acc_sc[h] / l_sc[h]).T


def _fox_prompt(qa, ka, vt, batch):
    _, t, da = qa.shape
    rows = FOX_ROWS
    n = t // batch // rows
    pairs = [(i, j) for i in range(n) for j in range(i + 1)]
    qi = jnp.asarray([p[0] for p in pairs], jnp.int32)
    kj = jnp.asarray([p[1] for p in pairs], jnp.int32)
    grid_spec = pltpu.PrefetchScalarGridSpec(
        num_scalar_prefetch=2,
        grid=(batch, len(pairs)),
        in_specs=[pl.BlockSpec((N_HEADS, rows, da), lambda b, p, qi, kj: (0, b * n + qi[p], 0)),
                  pl.BlockSpec((N_HEADS, rows, da), lambda b, p, qi, kj: (0, b * n + kj[p], 0)),
                  pl.BlockSpec((N_HEADS, HEAD_DIM, rows), lambda b, p, qi, kj: (0, 0, b * n + kj[p]))],
        out_specs=pl.BlockSpec((1, rows, D_ATT), lambda b, p, qi, kj: (b, qi[p], 0)),
        scratch_shapes=[pltpu.VMEM((N_HEADS, 1, rows), F32), pltpu.VMEM((N_HEADS, 1, rows), F32),
                        pltpu.VMEM((N_HEADS, HEAD_DIM, rows), F32)])
    return pl.pallas_call(
        functools.partial(_fox_prompt_kernel, rows=rows),
        grid_spec=grid_spec,
        out_shape=jax.ShapeDtypeStruct((batch, t // batch, D_ATT), F32),
        compiler_params=_params(("arbitrary", "arbitrary"), 48),
        name="fox_prompt",
    )(qi, kj, qa, ka, vt)


def _fox_sample_kernel(q_ref, kc_ref, vc_ref, kn_ref, vn_ref, fq_ref, fkc_ref, fkn_ref, o_ref,
                       m_sc, l_sc, acc_sc, *, t_new):
    j = pl.program_id(1)

    @pl.when(j == 0)
    def _():
        _flash_init(m_sc, l_sc, acc_sc)

    q = q_ref[0]
    qs = [(_head(q, h) * (SCALE * LOG2E)).astype(BF16) for h in range(N_HEADS)]
    fq = fq_ref[0] * LOG2E

    def bias(fk, h):
        return fq[:, h:h + 1] - fk[h:h + 1, :] * LOG2E

    fkc = fkc_ref[0]
    cached = [_head_rows(h, fkc.shape[1]) for h in range(N_HEADS)]
    _flash_step([_dot_nt(qs[h], kc_ref[cached[h], :].astype(BF16)) + bias(fkc, h) for h in range(N_HEADS)],
                [vc_ref[cached[h], :].astype(BF16) for h in range(N_HEADS)], m_sc, l_sc, acc_sc)

    @pl.when(j == pl.num_programs(1) - 1)
    def _():
        kn = kn_ref[0].astype(BF16)
        vn = vn_ref[0].astype(BF16)
        fkn = fkn_ref[0]
        causal = (lax.broadcasted_iota(jnp.int32, (t_new, t_new), 1)
                  <= lax.broadcasted_iota(jnp.int32, (t_new, t_new), 0))
        _flash_step([jnp.where(causal, _dot_nt(qs[h], _head(kn, h)) + bias(fkn, h), NEG) for h in range(N_HEADS)],
                    [_head(vn, h) for h in range(N_HEADS)], m_sc, l_sc, acc_sc)
        _flash_finish(o_ref, l_sc, acc_sc)


def _fox_sample(q, k_new, v_new, k_cache, v_cache, layer, fq, fk_cache, fk_new):
    b, t, d = q.shape
    p = k_cache.shape[2] // N_HEADS
    rows = min(FOX_CACHE_ROWS, p)
    new = pl.BlockSpec((1, t, d), lambda i, j: (i, 0, 0))
    old = pl.BlockSpec((None, None, rows * N_HEADS, HEAD_DIM), lambda i, j: (layer, i, j, 0))
    return pl.pallas_call(
        functools.partial(_fox_sample_kernel, t_new=t),
        grid=(b, p // rows),
        in_specs=[new, old, old, new, new,
                  pl.BlockSpec((1, t, N_HEADS), lambda i, j: (i, 0, 0)),
                  pl.BlockSpec((1, N_HEADS, rows), lambda i, j: (i, 0, j)),
                  pl.BlockSpec((1, N_HEADS, t), lambda i, j: (i, 0, 0))],
        out_specs=new,
        out_shape=jax.ShapeDtypeStruct((b, t, d), F32),
        scratch_shapes=_flash_scratch(t),
        compiler_params=_params(("arbitrary", "arbitrary"), 48),
        name="fox_sample",
    )(q, k_cache, v_cache, k_new, v_new, fq, fk_cache, fk_new)


def _mix_kernel(x_ref, ya_ref, yb_ref, yc_ref, g_ref, w_ref, lg_ref, lb_ref, o_ref, *, alpha):
    d_a = ya_ref.shape[-1]
    d_b = yb_ref.shape[-1]
    g = g_ref[...]
    y = jnp.concatenate([_rms_norm(ya_ref[...], g[:, :d_a]),
                         _rms_norm(yb_ref[...], g[:, d_a:d_a + d_b]),
                         _rms_norm(yc_ref[...], g[:, d_a + d_b:])], axis=1)
    z = alpha * x_ref[...] + jnp.dot(y.astype(BF16), w_ref[...], preferred_element_type=F32)
    o_ref[...] = _layer_norm(z, lg_ref[...], lb_ref[...])


def _mix(x2d, ya, yb, yc, g_mix, w_out_all, layer, ln_g, ln_b, alpha):
    t, d = x2d.shape
    rows = min(MIX_ROWS, t)
    tok = lambda w: pl.BlockSpec((rows, w), lambda i: (i, 0))
    vec = _resident((1, d))
    return pl.pallas_call(
        functools.partial(_mix_kernel, alpha=alpha),
        grid=(t // rows,),
        in_specs=[tok(d), tok(ya.shape[1]), tok(yb.shape[1]), tok(yc.shape[1]), vec,
                  _layer_block(layer, w_out_all.shape[1:]), vec, vec],
        out_specs=tok(d),
        out_shape=jax.ShapeDtypeStruct((t, d), F32),
        compiler_params=_params(("arbitrary",), 48),
        name="mix",
    )(x2d, ya, yb, yc, g_mix, w_out_all, ln_g, ln_b)


def _ffn_kernel(x_ref, wu_ref, wd_ref, lg_ref, lb_ref, o_ref, xb_sc, *, alpha):
    j = pl.program_id(1)
    last = pl.num_programs(1) - 1
    rows = o_ref.shape[0]
    part = rows // FFN_EDGE_PARTS
    parts = [slice(r * part, (r + 1) * part) for r in range(FFN_EDGE_PARTS)]

    def term(sl):
        hid = jnp.maximum(jnp.dot(xb_sc[sl, :], wu_ref[...], preferred_element_type=F32), 0.0)
        return jnp.dot((hid * hid).astype(BF16), wd_ref[...], preferred_element_type=F32)

    @pl.when(j == 0)
    def _():
        for sl in parts:
            xb_sc[sl, :] = x_ref[sl, :].astype(BF16)
            o_ref[sl, :] = term(sl)

    @pl.when(jnp.logical_and(j > 0, j < last))
    def _():
        o_ref[...] += term(slice(None))

    @pl.when(j == last)
    def _():
        for sl in parts:
            o_ref[sl, :] = _layer_norm(alpha * x_ref[sl, :] + (o_ref[sl, :] + term(sl)), lg_ref[...], lb_ref[...])


def _ffn(x2d, w_up_all, w_down_all, layer, ln_g, ln_b, alpha):
    t, d = x2d.shape
    d_ff = w_up_all.shape[2]
    rows, cols = min(FFN_ROWS, t), FFN_COLS
    vec = _resident((1, d))
    return pl.pallas_call(
        functools.partial(_ffn_kernel, alpha=alpha),
        grid=(t // rows, d_ff // cols),
        in_specs=[pl.BlockSpec((rows, d), lambda i, j: (i, 0)),
                  pl.BlockSpec((None, d, cols), lambda i, j: (layer, 0, j)),
                  pl.BlockSpec((None, cols, d), lambda i, j: (layer, j, 0)),
                  vec, vec],
        out_specs=pl.BlockSpec((rows, d), lambda i, j: (i, 0)),
        out_shape=jax.ShapeDtypeStruct((t, d), F32),
        scratch_shapes=[pltpu.VMEM((rows, d), BF16)],
        compiler_params=_params(("arbitrary", "arbitrary"), 56),
        name="ffn",
    )(x2d, w_up_all, w_down_all, ln_g, ln_b)


def _cum_logf(logf_t, off):
    b, h, t = logf_t.shape
    t_pad = -(-t // LANES) * LANES
    x = logf_t.reshape(b * h, t)
    if t_pad != t:
        x = jnp.pad(x, ((0, 0), (0, t_pad - t)))
    return _cumsum_lanes(x, off.reshape(b * h, 1))[:, :t].reshape(b, h, t)


def _row(v):
    return v.reshape(1, -1)


def kernel(x_prompt, x_sample, state_conv, state_lru, cache_band_k, cache_band_v, cache_fox_k, cache_fox_v,
           cache_fox_logf, w_in, w_conv, b_conv, w_rg_a, b_rg_a, w_rg_x, b_rg_x, lru_lambda, rel_bias,
           b_forget, g_mix, w_out, ln1_g, ln1_b, w_up, w_down, ln2_g, ln2_b):
    depth = w_in.shape[0]
    bp, s, d = x_prompt.shape
    bs, ts, _ = x_sample.shape
    d_a = w_conv.shape[-1]
    alpha = float((2 * depth) ** 0.25)
    d_main = 2 * d_a + 6 * D_ATT
    widths = (2 * d_a,) + (D_ATT,) * 6 + (LANES,)
    n_band_p = min(BAND, s)
    p_len = cache_fox_k.shape[2]

    w_in_b = jnp.transpose(w_in, (2, 0, 1)).reshape(w_in.shape[2], depth * d).astype(BF16)
    w_f_b = jnp.pad(w_in_b[d_main:], ((0, LANES - N_HEADS), (0, 0)))
    w_ra_b, w_rx_b = w_rg_a.astype(BF16), w_rg_x.astype(BF16)
    w_out_b, w_up_b, w_down_b = w_out.astype(BF16), w_up.astype(BF16), w_down.astype(BF16)

    xp = x_prompt.reshape(bp * s, d)
    xs = x_sample.reshape(bs * ts, d)
    outs = [[] for _ in range(8)]
    k5 = jnp.zeros((depth, bp, s * N_HEADS, HEAD_DIM), F32)
    v5 = jnp.zeros((depth, bp, s * N_HEADS, HEAD_DIM), F32)
    s_kv = [jnp.zeros((depth, bs, ts * N_HEADS, HEAD_DIM), F32) for _ in range(4)]
    band_kc, band_vc = _interleaved(cache_band_k), _interleaved(cache_band_v)
    fox_kc, fox_vc = _interleaved(cache_fox_k), _interleaved(cache_fox_v)

    for l in range(depth):
        bf_pad = jnp.pad(b_forget[l], (0, LANES - N_HEADS)).reshape(1, LANES)
        lru_w = (w_conv[l], _row(b_conv[l]), w_ra_b[l], _row(b_rg_a[l]), w_rx_b[l], _row(b_rg_x[l]),
                 _row(lru_lambda[l]))
        rel_row = _rel_row(rel_bias[l])

        def finish(x2d, ya, yb, yc):
            x1 = _mix(x2d, ya, yb, yc, _row(g_mix[l]), w_out_b, l, _row(ln1_g[l]), _row(ln1_b[l]), alpha)
            return _ffn(x1, w_up_b, w_down_b, l, _row(ln2_g[l]), _row(ln2_b[l]), alpha)

        a2, bq, bk, bv, lf, qa, ka, vt, k5, v5 = _proj_prompt(xp, w_in_b, w_f_b, l, bf_pad, widths, k5, v5, s)
        ya, buf, h_last = _rglru(a2.reshape(bp, s, 2 * d_a), jnp.zeros((bp, SUBLANES, d_a), F32),
                                 jnp.zeros((bp, 1, d_a), F32), *lru_w, reset_first=True)
        r3 = lambda z: z.reshape(bp, s, D_ATT)
        yb = _band_prompt(r3(bq), r3(bk), r3(bv), rel_row)
        yc = _fox_prompt(qa, ka, vt, bp)
        xp = finish(xp, ya.reshape(bp * s, d_a), yb.reshape(bp * s, D_ATT), yc.reshape(bp * s, D_ATT))
        h4 = lambda z, b_, t_: z.reshape(b_, t_, N_HEADS, HEAD_DIM)
        newest = lambda z: h4(r3(z)[:, s - n_band_p:], bp, n_band_p)
        for dst, val in zip(outs[:5], (buf[:, SUBLANES - (CONV_W - 1):], h_last.reshape(bp, d_a),
                                       newest(bk), newest(bv),
                                       lf[:, :N_HEADS].reshape(bp, s, N_HEADS))):
            dst.append(val)

        a2, bq, bk, bv, lf, cq, ck, cv, *s_kv = _proj_sample(xs, w_in_b, w_f_b, l, bf_pad, widths, s_kv, ts)
        conv8 = jnp.pad(state_conv[l], ((0, 0), (SUBLANES - (CONV_W - 1), 0), (0, 0)))
        ya, buf, h_last = _rglru(a2.reshape(bs, ts, 2 * d_a), conv8, state_lru[l].reshape(bs, 1, d_a),
                                 *lru_w, reset_first=False)
        r3 = lambda z: z.reshape(bs, ts, D_ATT)
        yb = _band_sample(r3(bq), r3(bk), r3(bv), band_kc, band_vc, l, rel_row)
        logf = lf[:, :N_HEADS].reshape(bs, ts, N_HEADS)
        fk_cache = _cum_logf(jnp.swapaxes(cache_fox_logf[l], 1, 2), jnp.zeros((bs, N_HEADS, 1), F32))
        fk_new = _cum_logf(jnp.swapaxes(logf, 1, 2), fk_cache[:, :, p_len - 1:])
        yc = _fox_sample(r3(cq), r3(ck), r3(cv), fox_kc, fox_vc, l,
                         jnp.swapaxes(fk_new, 1, 2), fk_cache, fk_new)
        xs = finish(xs, ya.reshape(bs * ts, d_a), yb.reshape(bs * ts, D_ATT), yc.reshape(bs * ts, D_ATT))
        for dst, val in zip(outs[5:], (buf[:, SUBLANES - (CONV_W - 1):], h_last.reshape(bs, d_a), logf)):
            dst.append(val)

    st = [jnp.stack(o, axis=0) for o in outs]
    heads_p = lambda z: z.reshape(depth, bp, s, N_HEADS, HEAD_DIM)
    heads_s = lambda z: z.reshape(depth, bs, ts, N_HEADS, HEAD_DIM)
    return (xp.reshape(bp, s, d), xs.reshape(bs, ts, d),
            st[0], st[1], st[2], st[3], heads_p(k5), heads_p(v5), st[4],
            st[5], st[6], *[heads_s(z) for z in s_kv], st[7])
```

```python
import functools

import numpy as np
import jax
import jax.numpy as jnp
from jax import lax
from jax.experimental import pallas as pl
from jax.experimental.pallas import tpu as pltpu

F32 = jnp.float32
BF16 = jnp.bfloat16

HEAD_DIM = 128
CHUNK = 64
N_PREV_CHUNKS = 8
BAND = N_PREV_CHUNKS * CHUNK
REL_CLIP = 256
CONV_W = 4
LRU_C = 8.0
LN_EPS = 1e-5
NEG = -1e30
SCALE = HEAD_DIM ** -0.5
LOG2E = float(np.log2(np.e))
N_HEADS = 4
D_ATT = N_HEADS * HEAD_DIM
LANES = 128
SUBLANES = 8
MIB = 1 << 20
N_SPLIT = 3
REL_ROW = 4 * BAND

PROJ_ROWS = 256
LRU_ROWS = 256
BAND_ROWS = 256
FOX_ROWS = 512
FOX_CACHE_ROWS = 2048
MIX_ROWS = 512
FFN_ROWS = 512
FFN_COLS = 1024
FFN_EDGE_PARTS = 4


def _params(semantics, vmem_mib):
    return pltpu.CompilerParams(dimension_semantics=semantics, vmem_limit_bytes=vmem_mib * MIB)


def _resident(shape):
    return pl.BlockSpec(shape, lambda *_: (0,) * len(shape), pipeline_mode=pl.Buffered(1))


def _layer_block(layer, shape):
    return pl.BlockSpec((None,) + tuple(shape), lambda *_: (layer,) + (0,) * len(shape),
                        pipeline_mode=pl.Buffered(1))


def _log_sigmoid(z):
    return jnp.minimum(z, 0.0) - jnp.log1p(jnp.exp(-jnp.abs(z)))


def _softplus(z):
    return jnp.maximum(z, 0.0) + jnp.log1p(jnp.exp(-jnp.abs(z)))


def _gelu_tanh(x):
    c = np.sqrt(2.0 / np.pi).astype(np.float32)
    return 0.5 * x * (1.0 + jnp.tanh(c * (x + 0.044715 * (x * x * x))))


def _layer_norm(z, g, b):
    mu = jnp.mean(z, axis=-1, keepdims=True)
    zc = z - mu
    var = jnp.mean(zc * zc, axis=-1, keepdims=True)
    return zc * lax.rsqrt(var + LN_EPS) * g + b


def _rms_norm(y, g):
    return y * lax.rsqrt(jnp.mean(y * y, axis=-1, keepdims=True) + LN_EPS) * g


def _dot_nt(a, b):
    return lax.dot_general(a, b, (((1,), (1,)), ((), ())), preferred_element_type=F32)


def _head(x, h):
    return x[:, h * HEAD_DIM:(h + 1) * HEAD_DIM]


def _head_rows(h, rows):
    return pl.ds(h, rows, stride=N_HEADS)


def _interleaved(a):
    return a.reshape(a.shape[:-3] + (a.shape[-3] * N_HEADS, HEAD_DIM))


def _cumsum_rows(x):
    row = lax.broadcasted_iota(jnp.int32, x.shape, 0)
    shift = 1
    while shift < x.shape[0]:
        x = jnp.where(row >= shift, x + pltpu.roll(x, shift, 0), x)
        shift *= 2
    return x


def _split_bf16(x):
    pieces = []
    for _ in range(N_SPLIT):
        p = x.astype(BF16).astype(F32)
        pieces.append(p)
        x = x - p
    return pieces


def _proj_kernel(*refs, widths, fox_aug, stream_rows, n_aliased):
    x_ref, w_ref, wf_ref, bf_ref = refs[:4]
    if fox_aug:
        (a2_ref, bq_ref, bk_ref, bv_ref, lf_ref,
         qa_ref, ka_ref, vt_ref, k5_ref, v5_ref, bk5_ref, bv5_ref, f_sc) = refs[4 + n_aliased:]
    else:
        (a2_ref, bq_ref, bk_ref, bv_ref, lf_ref, cq_ref, ck_ref, cv_ref,
         bk5_ref, bv5_ref, ck5_ref, cv5_ref) = refs[4 + n_aliased:]
    xb = x_ref[...].astype(BF16)
    rows = xb.shape[0]
    offs = np.concatenate([[0], np.cumsum(widths)])

    def group(g):
        return _dot_nt(xb, w_ref[int(offs[g]):int(offs[g + 1]), :])

    logf = _log_sigmoid(_dot_nt(xb, wf_ref[...]) + bf_ref[...])
    lf_ref[...] = logf
    cq, ck, cv = group(4), group(5), group(6)
    bq, bk, bv = group(1), group(2), group(3)
    a2_ref[...] = group(0)
    if not fox_aug:
        bq_ref[...] = bq
        bk_ref[...] = bk
        bv_ref[...] = bv
        cq_ref[...] = cq
        ck_ref[...] = ck
        cv_ref[...] = cv
        for dst_ref, val in ((bk5_ref, bk), (bv5_ref, bv), (ck5_ref, ck), (cv5_ref, cv)):
            for b in range(rows // stream_rows):
                for h in range(N_HEADS):
                    dst_ref[b, _head_rows(h, stream_rows), :] = _head(val[b * stream_rows:(b + 1) * stream_rows], h)
        return

    bq_ref[...] = (bq * (SCALE * LOG2E)).astype(BF16)
    bk_ref[...] = bk.astype(BF16)
    bv_ref[...] = bv.astype(BF16)
    blocks = stream_rows // rows
    block = pl.program_id(0) % blocks

    @pl.when(block >= blocks - min(BAND, stream_rows) // rows)
    def _():
        for h in range(N_HEADS):
            bk5_ref[_head_rows(h, rows), :] = _head(bk, h)
            bv5_ref[_head_rows(h, rows), :] = _head(bv, h)

    @pl.when(block == 0)
    def _():
        f_sc[...] = jnp.zeros(f_sc.shape, F32)

    cum = _cumsum_rows(logf) + f_sc[...]
    f_sc[...] = cum[rows - 1:rows]
    cum2 = cum * LOG2E
    lane = lax.broadcasted_iota(jnp.int32, (rows, HEAD_DIM), 1)
    for h in range(N_HEADS):
        pieces = _split_bf16(jnp.broadcast_to(cum2[:, h:h + 1], (rows, HEAD_DIM)))
        q_extra = jnp.where(lane < 2 * N_SPLIT, 1.0, 0.0)
        k_extra = jnp.where(lane < N_SPLIT, 1.0, 0.0)
        for n, piece in enumerate(pieces):
            q_extra = jnp.where(lane == n, piece, q_extra)
            k_extra = jnp.where(lane == N_SPLIT + n, -piece, k_extra)
        qa_ref[h, :, :HEAD_DIM] = (_head(cq, h) * (SCALE * LOG2E)).astype(BF16)
        qa_ref[h, :, HEAD_DIM:] = q_extra.astype(BF16)
        ka_ref[h, :, :HEAD_DIM] = _head(ck, h).astype(BF16)
        ka_ref[h, :, HEAD_DIM:] = k_extra.astype(BF16)
        vt_ref[h] = _head(cv, h).T.astype(BF16)
        k5_ref[_head_rows(h, rows), :] = _head(ck, h)
        v5_ref[_head_rows(h, rows), :] = _head(cv, h)


def _proj_sample(x2d, w_all, wf_all, layer, bf_pad, widths, kv_all, seq):
    t, d = x2d.shape
    rows = min(PROJ_ROWS, t)
    order = (0, 1, 2, 3, 7, 4, 5, 6)
    final = pl.BlockSpec((None, rows // seq, seq * N_HEADS, HEAD_DIM), lambda i: (layer, i, 0, 0))
    n_in = 4
    return pl.pallas_call(
        functools.partial(_proj_kernel, widths=widths, fox_aug=False, stream_rows=seq, n_aliased=len(kv_all)),
        grid=(t // rows,),
        in_specs=[pl.BlockSpec((rows, d), lambda i: (i, 0)), pl.BlockSpec((sum(widths[:-1]), d), lambda i: (0, layer), pipeline_mode=pl.Buffered(1)),
                  pl.BlockSpec((widths[-1], d), lambda i: (0, layer), pipeline_mode=pl.Buffered(1)), _resident((1, widths[-1]))]
                 + [pl.BlockSpec(memory_space=pl.ANY)] * len(kv_all),
        out_specs=[pl.BlockSpec((rows, widths[g]), lambda i: (i, 0)) for g in order] + [final] * len(kv_all),
        out_shape=[jax.ShapeDtypeStruct((t, widths[g]), F32) for g in order]
                  + [jax.ShapeDtypeStruct(a.shape, a.dtype) for a in kv_all],
        input_output_aliases={n_in + n: len(order) + n for n in range(len(kv_all))},
        compiler_params=_params(("arbitrary",), 48),
        name="proj_sample",
    )(x2d, w_all, wf_all, bf_pad, *kv_all)


def _proj_prompt(x2d, w_all, wf_all, layer, bf_pad, widths, kv_all, seq):
    t, d = x2d.shape
    rows = PROJ_ROWS
    bpb = seq // rows
    first_new = bpb - min(BAND, seq) // rows
    tok = lambda w, dt: (pl.BlockSpec((rows, w), lambda i: (i, 0)), jax.ShapeDtypeStruct((t, w), dt))
    aug = (pl.BlockSpec((N_HEADS, rows, 2 * HEAD_DIM), lambda i: (0, i, 0)),
           jax.ShapeDtypeStruct((N_HEADS, t, 2 * HEAD_DIM), BF16))
    every = pl.BlockSpec((None, None, rows * N_HEADS, HEAD_DIM), lambda i: (layer, i // bpb, i % bpb, 0))
    newest = pl.BlockSpec((None, None, rows * N_HEADS, HEAD_DIM),
                          lambda i: (layer, i // bpb, jnp.maximum(i % bpb - first_new, 0), 0))
    transposed = (pl.BlockSpec((N_HEADS, HEAD_DIM, rows), lambda i: (0, 0, i)),
                  jax.ShapeDtypeStruct((N_HEADS, HEAD_DIM, t), BF16))
    outs = [tok(widths[0], F32), tok(D_ATT, BF16), tok(D_ATT, BF16), tok(D_ATT, BF16), tok(widths[-1], F32),
            aug, aug, transposed] + [(spec, jax.ShapeDtypeStruct(a.shape, a.dtype))
                                     for spec, a in zip((every, every, newest, newest), kv_all)]
    n_in = 4
    return pl.pallas_call(
        functools.partial(_proj_kernel, widths=widths, fox_aug=True, stream_rows=seq, n_aliased=len(kv_all)),
        grid=(t // rows,),
        in_specs=[pl.BlockSpec((rows, d), lambda i: (i, 0)), pl.BlockSpec((sum(widths[:-1]), d), lambda i: (0, layer), pipeline_mode=pl.Buffered(1)),
                  pl.BlockSpec((widths[-1], d), lambda i: (0, layer), pipeline_mode=pl.Buffered(1)), _resident((1, widths[-1]))]
                 + [pl.BlockSpec(memory_space=pl.ANY)] * len(kv_all),
        out_specs=[o[0] for o in outs],
        out_shape=[o[1] for o in outs],
        scratch_shapes=[pltpu.VMEM((1, widths[-1]), F32)],
        input_output_aliases={n_in + n: len(outs) - len(kv_all) + n for n in range(len(kv_all))},
        compiler_params=_params(("arbitrary",), 52),
        name="proj_prompt",
    )(x2d, w_all, wf_all, bf_pad, *kv_all)


def _rglru_block(xa, ag, lru_refs, h_sc, tail_sc, first_block, reset_first):
    wc_ref, bc_ref, wra_ref, bra_ref, wrx_ref, brx_ref, lam_ref = lru_refs
    rows, d_a = xa.shape
    ext = jnp.concatenate([tail_sc[...], xa], axis=0)
    xc = bc_ref[...] + wc_ref[CONV_W - 1:CONV_W, :] * xa
    for j in range(1, CONV_W):
        shifted = pltpu.roll(ext, j, 0)[SUBLANES:SUBLANES + rows]
        xc = xc + wc_ref[CONV_W - 1 - j:CONV_W - j, :] * shifted
    tail_sc[...] = xa[rows - SUBLANES:rows]

    xcb = xc.astype(BF16)

    def gate(w_ref, b_ref):
        parts = [jnp.dot(xcb[:, n * HEAD_DIM:(n + 1) * HEAD_DIM], w_ref[n], preferred_element_type=F32)
                 for n in range(d_a // HEAD_DIM)]
        return jax.nn.sigmoid(jnp.concatenate(parts, axis=1) + b_ref[...])

    r = gate(wra_ref, bra_ref)
    i_gate = gate(wrx_ref, brx_ref)
    log_a = (-LRU_C) * r * _softplus(-lam_ref[...])
    a = jnp.exp(log_a)
    mult = jnp.exp(0.5 * jnp.log(1.0 - a * a))
    row = lax.broadcasted_iota(jnp.int32, (rows, d_a), 0)
    if reset_first:
        mult = jnp.where(jnp.logical_and(row == 0, first_block), 1.0, mult)
    u = mult * (i_gate * xc)

    n_groups = rows // SUBLANES
    a = a.reshape(n_groups, SUBLANES, d_a)
    u = u.reshape(n_groups, SUBLANES, d_a)
    in_group = lax.broadcasted_iota(jnp.int32, a.shape, 1)
    shift = 1
    while shift < SUBLANES:
        a_prev = pltpu.roll(a, shift, 1)
        u_prev = pltpu.roll(u, shift, 1)
        live = in_group >= shift
        u = jnp.where(live, a * u_prev + u, u)
        a = jnp.where(live, a * a_prev, a)
        shift *= 2
    h_last = h_sc[...]
    groups = []
    for g in range(n_groups):
        h_g = u[g] + a[g] * h_last
        h_last = h_g[SUBLANES - 1:SUBLANES]
        groups.append(h_g)
    h = jnp.concatenate(groups, axis=0)
    h_sc[...] = h_last
    return h * _gelu_tanh(ag), h_last


def _rglru_kernel(ax_ref, ag_ref, cs_ref, h0_ref, *rest, rows, reset_first):
    lru_refs, (y_ref, buf_ref, hl_ref, h_sc, tail_sc) = rest[:7], rest[7:]
    t = pl.program_id(1)

    @pl.when(t == 0)
    def _():
        h_sc[...] = h0_ref[0]
        tail_sc[...] = cs_ref[0]

    xa = ax_ref[0]
    y_ref[0], h_last = _rglru_block(xa, ag_ref[0], lru_refs, h_sc, tail_sc, t == 0, reset_first)

    @pl.when(t == pl.num_programs(1) - 1)
    def _():
        buf_ref[0] = xa[rows - SUBLANES:rows]
        hl_ref[0] = h_last


def _rglru(a2, conv_state8, h0, w_conv, b_conv, w_ra, b_ra, w_rx, b_rx, lam, *, reset_first):
    b, t, d2 = a2.shape
    d_a = d2 // 2
    rows = min(LRU_ROWS, t)
    n_blocks = d_a // HEAD_DIM
    row_vec = _resident((1, d_a))
    gate_w = _resident((n_blocks, HEAD_DIM, HEAD_DIM))
    kern = functools.partial(_rglru_kernel, rows=rows, reset_first=reset_first)
    return pl.pallas_call(
        kern,
        grid=(b, t // rows),
        in_specs=[pl.BlockSpec((1, rows, d_a), lambda i, j: (i, j, 0)),
                  pl.BlockSpec((1, rows, d_a), lambda i, j: (i, j, 1)),
                  pl.BlockSpec((1, SUBLANES, d_a), lambda i, j: (i, 0, 0)),
                  pl.BlockSpec((1, 1, d_a), lambda i, j: (i, 0, 0)),
                  _resident((CONV_W, d_a)), row_vec, gate_w, row_vec, gate_w, row_vec, row_vec],
        out_specs=[pl.BlockSpec((1, rows, d_a), lambda i, j: (i, j, 0)),
                   pl.BlockSpec((1, SUBLANES, d_a), lambda i, j: (i, 0, 0)),
                   pl.BlockSpec((1, 1, d_a), lambda i, j: (i, 0, 0))],
        out_shape=[jax.ShapeDtypeStruct((b, t, d_a), F32),
                   jax.ShapeDtypeStruct((b, SUBLANES, d_a), F32),
                   jax.ShapeDtypeStruct((b, 1, d_a), F32)],
        scratch_shapes=[pltpu.VMEM((1, d_a), F32), pltpu.VMEM((SUBLANES, d_a), F32)],
        compiler_params=_params(("arbitrary", "arbitrary"), 48),
        name="rglru_reset" if reset_first else "rglru_carry",
    )(a2, a2, conv_state8, h0, w_conv, b_conv, w_ra, b_ra, w_rx, b_rx, lam)


def _cumsum_kernel(x_ref, off_ref, o_ref):
    x = x_ref[...]
    n = x.shape[1]
    lane = lax.broadcasted_iota(jnp.int32, x.shape, 1)
    shift = 1
    while shift < n:
        prev = pltpu.roll(x, shift, 1)
        x = jnp.where(lane >= shift, x + prev, x)
        shift *= 2
    o_ref[...] = x + off_ref[...]


def _cumsum_lanes(x, off):
    r, n = x.shape
    return pl.pallas_call(
        _cumsum_kernel,
        grid=(1,),
        in_specs=[pl.BlockSpec((r, n), lambda i: (0, 0)), pl.BlockSpec((r, 1), lambda i: (0, 0))],
        out_specs=pl.BlockSpec((r, n), lambda i: (0, 0)),
        out_shape=jax.ShapeDtypeStruct((r, n), F32),
        compiler_params=_params(("arbitrary",), 32),
        name="cumsum_lanes",
    )(x, off)


def _softmax_pv(s, v_h):
    m = jnp.max(s, axis=-1, keepdims=True)
    p = jnp.exp2(s - m)
    den = jnp.sum(p, axis=-1, keepdims=True)
    return jnp.dot(p.astype(BF16), v_h, preferred_element_type=F32) / den


def _flash_step(scores, values, m_sc, l_sc, acc_sc):
    prev = [(m_sc[h], l_sc[h], acc_sc[h]) for h in range(N_HEADS)]
    new = []
    for (m_prev, l_prev, acc_prev), s, v_h in zip(prev, scores, values):
        m_new = jnp.maximum(m_prev, jnp.max(s, axis=-1, keepdims=True))
        alpha = jnp.exp2(m_prev - m_new)
        p = jnp.exp2(s - m_new)
        new.append((m_new, alpha * l_prev + jnp.sum(p, axis=-1, keepdims=True),
                    alpha * acc_prev + jnp.dot(p.astype(BF16), v_h, preferred_element_type=F32)))
    for h, (m_new, l_new, acc_new) in enumerate(new):
        m_sc[h] = m_new
        l_sc[h] = l_new
        acc_sc[h] = acc_new


def _flash_init(m_sc, l_sc, acc_sc):
    m_sc[...] = jnp.full(m_sc.shape, NEG, F32)
    l_sc[...] = jnp.zeros(l_sc.shape, F32)
    acc_sc[...] = jnp.zeros(acc_sc.shape, F32)


def _flash_finish(o_ref, l_sc, acc_sc):
    for h in range(N_HEADS):
        o_ref[0, :, h * HEAD_DIM:(h + 1) * HEAD_DIM] = acc_sc[h] / l_sc[h]


def _flash_scratch(rows):
    return [pltpu.VMEM((N_HEADS, rows, 1), F32), pltpu.VMEM((N_HEADS, rows, 1), F32),
            pltpu.VMEM((N_HEADS, rows, HEAD_DIM), F32)]


def _rel_row(rel_table):
    near = rel_table[:, 2 * REL_CLIP:]
    far = rel_table[:, :1]
    n_far = REL_ROW // 2 - BAND - REL_CLIP - 1
    row = jnp.concatenate([jnp.broadcast_to(near, (N_HEADS, BAND - REL_CLIP)), rel_table[:, ::-1],
                           jnp.broadcast_to(far, (N_HEADS, n_far)),
                           jnp.broadcast_to(near, (N_HEADS, REL_ROW // 2))], axis=1)
    return row.reshape(N_HEADS, 1, REL_ROW).astype(F32)


def _rel_tile(row_ref, h, rows, cols):
    tile = pltpu.roll(jnp.broadcast_to(row_ref[h], (rows, REL_ROW)), 0, 1, stride=1, stride_axis=0)
    return tile[:, :cols] * LOG2E


def _band_prompt_kernel(*refs, rows, n_prev):
    n_kv = n_prev + 1
    q_ref, k_refs, v_refs = refs[0], refs[1:1 + n_kv], refs[1 + n_kv:1 + 2 * n_kv]
    row_ref, o_ref, bias_sc = refs[1 + 2 * n_kv:]
    cols = n_kv * rows
    i = pl.program_id(1)

    @pl.when(jnp.logical_and(pl.program_id(0) == 0, i == 0))
    def _():
        qc = lax.broadcasted_iota(jnp.int32, (rows, cols), 0) // CHUNK
        kc = lax.broadcasted_iota(jnp.int32, (rows, cols), 1) // CHUNK - (n_prev * rows) // CHUNK
        visible = jnp.logical_and(kc <= qc, kc >= qc - N_PREV_CHUNKS)
        for h in range(N_HEADS):
            bias_sc[h] = jnp.where(visible, _rel_tile(row_ref, h, rows, cols), NEG)

    def attend(clip_start):
        q = q_ref[0]
        k = jnp.concatenate([r[0] for r in k_refs], axis=0)
        v = jnp.concatenate([r[0] for r in v_refs], axis=0)
        for h in range(N_HEADS):
            s = _dot_nt(_head(q, h), _head(k, h)) + bias_sc[h]
            if clip_start:
                col = lax.broadcasted_iota(jnp.int32, (rows, cols), 1)
                s = jnp.where(col >= (n_prev - i) * rows, s, NEG)
            o_ref[0, :, h * HEAD_DIM:(h + 1) * HEAD_DIM] = _softmax_pv(s, _head(v, h))

    @pl.when(i < n_prev)
    def _():
        attend(True)

    @pl.when(i >= n_prev)
    def _():
        attend(False)


def _band_prompt(q, k, v, rel_row):
    b, s, d = q.shape
    rows = BAND_ROWS
    n_prev = BAND // rows
    assert BAND == n_prev * rows and s % rows == 0 and rows + BAND <= REL_ROW // 2
    back = lambda steps: pl.BlockSpec((1, rows, d), lambda i, j: (i, jnp.maximum(j - steps, 0), 0))
    window = [back(steps) for steps in range(n_prev, -1, -1)]
    return pl.pallas_call(
        functools.partial(_band_prompt_kernel, rows=rows, n_prev=n_prev),
        grid=(b, s // rows),
        in_specs=[back(0)] + window + window + [_resident(rel_row.shape)],
        out_specs=back(0),
        out_shape=jax.ShapeDtypeStruct((b, s, d), F32),
        scratch_shapes=[pltpu.VMEM((N_HEADS, rows, (n_prev + 1) * rows), F32)],
        compiler_params=_params(("arbitrary", "arbitrary"), 48),
        name="band_prompt",
    )(q, *[k] * (n_prev + 1), *[v] * (n_prev + 1), rel_row)


def _band_sample_kernel(q_ref, kn_ref, vn_ref, kc_ref, vc_ref, row_ref, o_ref, bias_sc, *, t_new, n_cache):
    @pl.when(pl.program_id(0) == 0)
    def _():
        for h in range(N_HEADS):
            bias_sc[h] = _rel_tile(row_ref, h, t_new, n_cache + t_new)

    q = q_ref[0]
    kn = kn_ref[0].astype(BF16)
    vn = vn_ref[0].astype(BF16)
    for h in range(N_HEADS):
        cached = _head_rows(h, n_cache)
        k = jnp.concatenate([kc_ref[cached, :].astype(BF16), _head(kn, h)], axis=0)
        v = jnp.concatenate([vc_ref[cached, :].astype(BF16), _head(vn, h)], axis=0)
        s = _dot_nt((_head(q, h) * (SCALE * LOG2E)).astype(BF16), k) + bias_sc[h]
        o_ref[0, :, h * HEAD_DIM:(h + 1) * HEAD_DIM] = _softmax_pv(s, v)


def _band_sample(q, k_new, v_new, k_cache, v_cache, layer, rel_row):
    b, t, d = q.shape
    n_c = k_cache.shape[2] // N_HEADS
    assert n_c == BAND
    new = pl.BlockSpec((1, t, d), lambda i: (i, 0, 0))
    old = pl.BlockSpec((None, None, n_c * N_HEADS, HEAD_DIM), lambda i: (layer, i, 0, 0))
    return pl.pallas_call(
        functools.partial(_band_sample_kernel, t_new=t, n_cache=n_c),
        grid=(b,),
        in_specs=[new, new, new, old, old, _resident(rel_row.shape)],
        out_specs=new,
        out_shape=jax.ShapeDtypeStruct((b, t, d), F32),
        scratch_shapes=[pltpu.VMEM((N_HEADS, t, n_c + t), F32)],
        compiler_params=_params(("arbitrary",), 32),
        name="band_sample",
    )(q, k_new, v_new, k_cache, v_cache, rel_row)


def _fox_prompt_kernel(qi_ref, kj_ref, qa_ref, ka_ref, vt_ref, o_ref, m_sc, l_sc, acc_sc, *, rows):
    i = qi_ref[pl.program_id(1)]
    j = kj_ref[pl.program_id(1)]

    @pl.when(j == 0)
    def _():
        _flash_init(m_sc, l_sc, acc_sc)

    def attend(diagonal):
        scores = [_dot_nt(ka_ref[h], qa_ref[h]) for h in range(N_HEADS)]
        if diagonal:
            causal = (lax.broadcasted_iota(jnp.int32, (rows, rows), 0)
                      <= lax.broadcasted_iota(jnp.int32, (rows, rows), 1))
            scores = [jnp.where(causal, s, NEG) for s in scores]
        prev = [(m_sc[h], l_sc[h], acc_sc[h]) for h in range(N_HEADS)]
        new = []
        for h, ((m_prev, l_prev, acc_prev), s) in enumerate(zip(prev, scores)):
            m_new = jnp.maximum(m_prev, jnp.max(s, axis=0, keepdims=True))
            alpha = jnp.exp2(m_prev - m_new)
            p = jnp.exp2(s - m_new)
            new.append((m_new, alpha * l_prev + jnp.sum(p, axis=0, keepdims=True),
                        alpha * acc_prev + jnp.dot(vt_ref[h], p.astype(BF16), preferred_element_type=F32)))
        for h, (m_new, l_new, acc_new) in enumerate(new):
            m_sc[h] = m_new
            l_sc[h] = l_new
            acc_sc[h] = acc_new

    @pl.when(j < i)
    def _():
        attend(False)

    @pl.when(j == i)
    def _():
        attend(True)
        for h in range(N_HEADS):
            o_ref[0, :, h * HEAD_DIM:(h + 1) * HEAD_DIM] = (acc_sc[h] / l_sc[h]).T


def _fox_prompt(qa, ka, vt, batch):
    _, t, da = qa.shape
    rows = FOX_ROWS
    n = t // batch // rows
    pairs = [(i, j) for i in range(n) for j in range(i + 1)]
    qi = jnp.asarray([p[0] for p in pairs], jnp.int32)
    kj = jnp.asarray([p[1] for p in pairs], jnp.int32)
    grid_spec = pltpu.PrefetchScalarGridSpec(
        num_scalar_prefetch=2,
        grid=(batch, len(pairs)),
        in_specs=[pl.BlockSpec((N_HEADS, rows, da), lambda b, p, qi, kj: (0, b * n + qi[p], 0)),
                  pl.BlockSpec((N_HEADS, rows, da), lambda b, p, qi, kj: (0, b * n + kj[p], 0)),
                  pl.BlockSpec((N_HEADS, HEAD_DIM, rows), lambda b, p, qi, kj: (0, 0, b * n + kj[p]))],
        out_specs=pl.BlockSpec((1, rows, D_ATT), lambda b, p, qi, kj: (b, qi[p], 0)),
        scratch_shapes=[pltpu.VMEM((N_HEADS, 1, rows), F32), pltpu.VMEM((N_HEADS, 1, rows), F32),
                        pltpu.VMEM((N_HEADS, HEAD_DIM, rows), F32)])
    return pl.pallas_call(
        functools.partial(_fox_prompt_kernel, rows=rows),
        grid_spec=grid_spec,
        out_shape=jax.ShapeDtypeStruct((batch, t // batch, D_ATT), F32),
        compiler_params=_params(("arbitrary", "arbitrary"), 48),
        name="fox_prompt",
    )(qi, kj, qa, ka, vt)


def _fox_sample_kernel(q_ref, kc_ref, vc_ref, kn_ref, vn_ref, fq_ref, fkc_ref, fkn_ref, o_ref,
                       m_sc, l_sc, acc_sc, *, t_new):
    j = pl.program_id(1)

    @pl.when(j == 0)
    def _():
        _flash_init(m_sc, l_sc, acc_sc)

    q = q_ref[0]
    qs = [(_head(q, h) * (SCALE * LOG2E)).astype(BF16) for h in range(N_HEADS)]
    fq = fq_ref[0] * LOG2E

    def bias(fk, h):
        return fq[:, h:h + 1] - fk[h:h + 1, :] * LOG2E

    fkc = fkc_ref[0]
    cached = [_head_rows(h, fkc.shape[1]) for h in range(N_HEADS)]
    _flash_step([_dot_nt(qs[h], kc_ref[cached[h], :].astype(BF16)) + bias(fkc, h) for h in range(N_HEADS)],
                [vc_ref[cached[h], :].astype(BF16) for h in range(N_HEADS)], m_sc, l_sc, acc_sc)

    @pl.when(j == pl.num_programs(1) - 1)
    def _():
        kn = kn_ref[0].astype(BF16)
        vn = vn_ref[0].astype(BF16)
        fkn = fkn_ref[0]
        causal = (lax.broadcasted_iota(jnp.int32, (t_new, t_new), 1)
                  <= lax.broadcasted_iota(jnp.int32, (t_new, t_new), 0))
        _flash_step([jnp.where(causal, _dot_nt(qs[h], _head(kn, h)) + bias(fkn, h), NEG) for h in range(N_HEADS)],
                    [_head(vn, h) for h in range(N_HEADS)], m_sc, l_sc, acc_sc)
        _flash_finish(o_ref, l_sc, acc_sc)


def _fox_sample(q, k_new, v_new, k_cache, v_cache, layer, fq, fk_cache, fk_new):
    b, t, d = q.shape
    p = k_cache.shape[2] // N_HEADS
    rows = min(FOX_CACHE_ROWS, p)
    new = pl.BlockSpec((1, t, d), lambda i, j: (i, 0, 0))
    old = pl.BlockSpec((None, None, rows * N_HEADS, HEAD_DIM), lambda i, j: (layer, i, j, 0))
    return pl.pallas_call(
        functools.partial(_fox_sample_kernel, t_new=t),
        grid=(b, p // rows),
        in_specs=[new, old, old, new, new,
                  pl.BlockSpec((1, t, N_HEADS), lambda i, j: (i, 0, 0)),
                  pl.BlockSpec((1, N_HEADS, rows), lambda i, j: (i, 0, j)),
                  pl.BlockSpec((1, N_HEADS, t), lambda i, j: (i, 0, 0))],
        out_specs=new,
        out_shape=jax.ShapeDtypeStruct((b, t, d), F32),
        scratch_shapes=_flash_scratch(t),
        compiler_params=_params(("arbitrary", "arbitrary"), 48),
        name="fox_sample",
    )(q, k_cache, v_cache, k_new, v_new, fq, fk_cache, fk_new)


def _mix_kernel(x_ref, ya_ref, yb_ref, yc_ref, g_ref, w_ref, lg_ref, lb_ref, o_ref, *, alpha):
    d_a = ya_ref.shape[-1]
    d_b = yb_ref.shape[-1]
    g = g_ref[...]
    y = jnp.concatenate([_rms_norm(ya_ref[...], g[:, :d_a]),
                         _rms_norm(yb_ref[...], g[:, d_a:d_a + d_b]),
                         _rms_norm(yc_ref[...], g[:, d_a + d_b:])], axis=1)
    z = alpha * x_ref[...] + jnp.dot(y.astype(BF16), w_ref[...], preferred_element_type=F32)
    o_ref[...] = _layer_norm(z, lg_ref[...], lb_ref[...])


def _mix(x2d, ya, yb, yc, g_mix, w_out_all, layer, ln_g, ln_b, alpha):
    t, d = x2d.shape
    rows = min(MIX_ROWS, t)
    tok = lambda w: pl.BlockSpec((rows, w), lambda i: (i, 0))
    vec = _resident((1, d))
    return pl.pallas_call(
        functools.partial(_mix_kernel, alpha=alpha),
        grid=(t // rows,),
        in_specs=[tok(d), tok(ya.shape[1]), tok(yb.shape[1]), tok(yc.shape[1]), vec,
                  _layer_block(layer, w_out_all.shape[1:]), vec, vec],
        out_specs=tok(d),
        out_shape=jax.ShapeDtypeStruct((t, d), F32),
        compiler_params=_params(("arbitrary",), 48),
        name="mix",
    )(x2d, ya, yb, yc, g_mix, w_out_all, ln_g, ln_b)


def _ffn_kernel(x_ref, wu_ref, wd_ref, lg_ref, lb_ref, o_ref, xb_sc, *, alpha):
    j = pl.program_id(1)
    last = pl.num_programs(1) - 1
    rows = o_ref.shape[0]
    part = rows // FFN_EDGE_PARTS
    parts = [slice(r * part, (r + 1) * part) for r in range(FFN_EDGE_PARTS)]

    def term(sl):
        hid = jnp.maximum(jnp.dot(xb_sc[sl, :], wu_ref[...], preferred_element_type=F32), 0.0)
        return jnp.dot((hid * hid).astype(BF16), wd_ref[...], preferred_element_type=F32)

    @pl.when(j == 0)
    def _():
        for sl in parts:
            xb_sc[sl, :] = x_ref[sl, :].astype(BF16)
            o_ref[sl, :] = term(sl)

    @pl.when(jnp.logical_and(j > 0, j < last))
    def _():
        o_ref[...] += term(slice(None))

    @pl.when(j == last)
    def _():
        for sl in parts:
            o_ref[sl, :] = _layer_norm(alpha * x_ref[sl, :] + (o_ref[sl, :] + term(sl)), lg_ref[...], lb_ref[...])


def _ffn(x2d, w_up_all, w_down_all, layer, ln_g, ln_b, alpha):
    t, d = x2d.shape
    d_ff = w_up_all.shape[2]
    rows, cols = min(FFN_ROWS, t), FFN_COLS
    vec = _resident((1, d))
    return pl.pallas_call(
        functools.partial(_ffn_kernel, alpha=alpha),
        grid=(t // rows, d_ff // cols),
        in_specs=[pl.BlockSpec((rows, d), lambda i, j: (i, 0)),
                  pl.BlockSpec((None, d, cols), lambda i, j: (layer, 0, j)),
                  pl.BlockSpec((None, cols, d), lambda i, j: (layer, j, 0)),
                  vec, vec],
        out_specs=pl.BlockSpec((rows, d), lambda i, j: (i, 0)),
        out_shape=jax.ShapeDtypeStruct((t, d), F32),
        scratch_shapes=[pltpu.VMEM((rows, d), BF16)],
        compiler_params=_params(("arbitrary", "arbitrary"), 56),
        name="ffn",
    )(x2d, w_up_all, w_down_all, ln_g, ln_b)


def _cum_logf(logf_t, off):
    b, h, t = logf_t.shape
    t_pad = -(-t // LANES) * LANES
    x = logf_t.reshape(b * h, t)
    if t_pad != t:
        x = jnp.pad(x, ((0, 0), (0, t_pad - t)))
    return _cumsum_lanes(x, off.reshape(b * h, 1))[:, :t].reshape(b, h, t)


def _row(v):
    return v.reshape(1, -1)


def kernel(x_prompt, x_sample, state_conv, state_lru, cache_band_k, cache_band_v, cache_fox_k, cache_fox_v,
           cache_fox_logf, w_in, w_conv, b_conv, w_rg_a, b_rg_a, w_rg_x, b_rg_x, lru_lambda, rel_bias,
           b_forget, g_mix, w_out, ln1_g, ln1_b, w_up, w_down, ln2_g, ln2_b):
    depth = w_in.shape[0]
    bp, s, d = x_prompt.shape
    bs, ts, _ = x_sample.shape
    d_a = w_conv.shape[-1]
    alpha = float((2 * depth) ** 0.25)
    d_main = 2 * d_a + 6 * D_ATT
    widths = (2 * d_a,) + (D_ATT,) * 6 + (LANES,)
    n_band_p = min(BAND, s)
    p_len = cache_fox_k.shape[2]

    w_in_t = jnp.transpose(w_in, (2, 0, 1))
    w_in_b = w_in_t[:d_main].reshape(d_main, depth * d).astype(BF16)
    w_f_b = jnp.pad(w_in_t[d_main:].reshape(N_HEADS, depth * d),
                    ((0, LANES - N_HEADS), (0, 0))).astype(BF16)
    w_ra_b, w_rx_b = w_rg_a.astype(BF16), w_rg_x.astype(BF16)
    w_out_b, w_up_b, w_down_b = w_out.astype(BF16), w_up.astype(BF16), w_down.astype(BF16)

    xp = x_prompt.reshape(bp * s, d)
    xs = x_sample.reshape(bs * ts, d)
    outs = [[] for _ in range(6)]
    p_kv = [jnp.zeros((depth, bp, rows_ * N_HEADS, HEAD_DIM), F32) for rows_ in (s, s, n_band_p, n_band_p)]
    s_kv = [jnp.zeros((depth, bs, ts * N_HEADS, HEAD_DIM), F32) for _ in range(4)]
    band_kc, band_vc = _interleaved(cache_band_k), _interleaved(cache_band_v)
    fox_kc, fox_vc = _interleaved(cache_fox_k), _interleaved(cache_fox_v)

    for l in range(depth):
        bf_pad = jnp.pad(b_forget[l], (0, LANES - N_HEADS)).reshape(1, LANES)
        lru_w = (w_conv[l], _row(b_conv[l]), w_ra_b[l], _row(b_rg_a[l]), w_rx_b[l], _row(b_rg_x[l]),
                 _row(lru_lambda[l]))
        rel_row = _rel_row(rel_bias[l])

        def finish(x2d, ya, yb, yc):
            x1 = _mix(x2d, ya, yb, yc, _row(g_mix[l]), w_out_b, l, _row(ln1_g[l]), _row(ln1_b[l]), alpha)
            return _ffn(x1, w_up_b, w_down_b, l, _row(ln2_g[l]), _row(ln2_b[l]), alpha)

        a2, bq, bk, bv, lf, qa, ka, vt, *p_kv = _proj_prompt(xp, w_in_b, w_f_b, l, bf_pad, widths, p_kv, s)
        ya, buf, h_last = _rglru(a2.reshape(bp, s, 2 * d_a), jnp.zeros((bp, SUBLANES, d_a), F32),
                                 jnp.zeros((bp, 1, d_a), F32), *lru_w, reset_first=True)
        r3 = lambda z: z.reshape(bp, s, D_ATT)
        yb = _band_prompt(r3(bq), r3(bk), r3(bv), rel_row)
        yc = _fox_prompt(qa, ka, vt, bp)
        xp = finish(xp, ya.reshape(bp * s, d_a), yb.reshape(bp * s, D_ATT), yc.reshape(bp * s, D_ATT))
        for dst, val in zip(outs[:3], (buf[:, SUBLANES - (CONV_W - 1):], h_last.reshape(bp, d_a),
                                       lf[:, :N_HEADS].reshape(bp, s, N_HEADS))):
            dst.append(val)

        a2, bq, bk, bv, lf, cq, ck, cv, *s_kv = _proj_sample(xs, w_in_b, w_f_b, l, bf_pad, widths, s_kv, ts)
        conv8 = jnp.pad(state_conv[l], ((0, 0), (SUBLANES - (CONV_W - 1), 0), (0, 0)))
        ya, buf, h_last = _rglru(a2.reshape(bs, ts, 2 * d_a), conv8, state_lru[l].reshape(bs, 1, d_a),
                                 *lru_w, reset_first=False)
        r3 = lambda z: z.reshape(bs, ts, D_ATT)
        yb = _band_sample(r3(bq), r3(bk), r3(bv), band_kc, band_vc, l, rel_row)
        logf = lf[:, :N_HEADS].reshape(bs, ts, N_HEADS)
        fk_cache = _cum_logf(jnp.swapaxes(cache_fox_logf[l], 1, 2), jnp.zeros((bs, N_HEADS, 1), F32))
        fk_new = _cum_logf(jnp.swapaxes(logf, 1, 2), fk_cache[:, :, p_len - 1:])
        yc = _fox_sample(r3(cq), r3(ck), r3(cv), fox_kc, fox_vc, l,
                         jnp.swapaxes(fk_new, 1, 2), fk_cache, fk_new)
        xs = finish(xs, ya.reshape(bs * ts, d_a), yb.reshape(bs * ts, D_ATT), yc.reshape(bs * ts, D_ATT))
        for dst, val in zip(outs[3:], (buf[:, SUBLANES - (CONV_W - 1):], h_last.reshape(bs, d_a), logf)):
            dst.append(val)

    st = [jnp.stack(o, axis=0) for o in outs]
    heads = lambda z: z.reshape(z.shape[:2] + (z.shape[2] // N_HEADS, N_HEADS, HEAD_DIM))
    fox_k, fox_v, band_k, band_v = [heads(z) for z in p_kv]
    return (xp.reshape(bp, s, d), xs.reshape(bs, ts, d),
            st[0], st[1], band_k, band_v, fox_k, fox_v, st[2],
            st[3], st[4], *[heads(z) for z in s_kv], st[5])
```

```python
import functools

import numpy as np
import jax
import jax.numpy as jnp
from jax import lax
from jax.experimental import pallas as pl
from jax.experimental.pallas import tpu as pltpu

F32 = jnp.float32
BF16 = jnp.bfloat16

HEAD_DIM = 128
CHUNK = 64
N_PREV_CHUNKS = 8
BAND = N_PREV_CHUNKS * CHUNK
REL_CLIP = 256
CONV_W = 4
LRU_C = 8.0
LN_EPS = 1e-5
NEG = -1e30
SCALE = HEAD_DIM ** -0.5
LOG2E = float(np.log2(np.e))
N_HEADS = 4
D_ATT = N_HEADS * HEAD_DIM
LANES = 128
SUBLANES = 8
MIB = 1 << 20
N_SPLIT = 3
REL_ROW = 4 * BAND

PROJ_ROWS = 256
LRU_ROWS = 256
BAND_ROWS = 256
FOX_ROWS = 512
FOX_CACHE_ROWS = 2048
FOX_CACHE_PARTS = 4
MIX_ROWS = 512
FFN_ROWS = 512
FFN_COLS = 1024
FFN_EDGE_PARTS = 4


def _params(semantics, vmem_mib):
    return pltpu.CompilerParams(dimension_semantics=semantics, vmem_limit_bytes=vmem_mib * MIB)


def _resident(shape):
    return pl.BlockSpec(shape, lambda *_: (0,) * len(shape), pipeline_mode=pl.Buffered(1))


def _layer_block(layer, shape):
    return pl.BlockSpec((None,) + tuple(shape), lambda *_: (layer,) + (0,) * len(shape),
                        pipeline_mode=pl.Buffered(1))


def _log_sigmoid(z):
    return jnp.minimum(z, 0.0) - jnp.log1p(jnp.exp(-jnp.abs(z)))


def _softplus(z):
    return jnp.maximum(z, 0.0) + jnp.log1p(jnp.exp(-jnp.abs(z)))


def _gelu_tanh(x):
    c = np.sqrt(2.0 / np.pi).astype(np.float32)
    return 0.5 * x * (1.0 + jnp.tanh(c * (x + 0.044715 * (x * x * x))))


def _layer_norm(z, g, b):
    mu = jnp.mean(z, axis=-1, keepdims=True)
    zc = z - mu
    var = jnp.mean(zc * zc, axis=-1, keepdims=True)
    return zc * lax.rsqrt(var + LN_EPS) * g + b


def _rms_norm(y, g):
    return y * lax.rsqrt(jnp.mean(y * y, axis=-1, keepdims=True) + LN_EPS) * g


def _dot_nt(a, b):
    return lax.dot_general(a, b, (((1,), (1,)), ((), ())), preferred_element_type=F32)


def _head(x, h):
    return x[:, h * HEAD_DIM:(h + 1) * HEAD_DIM]


def _head_rows(h, rows):
    return pl.ds(h, rows, stride=N_HEADS)


def _interleaved(a):
    return a.reshape(a.shape[:-3] + (a.shape[-3] * N_HEADS, HEAD_DIM))


def _cumsum_rows(x):
    row = lax.broadcasted_iota(jnp.int32, x.shape, 0)
    shift = 1
    while shift < x.shape[0]:
        x = jnp.where(row >= shift, x + pltpu.roll(x, shift, 0), x)
        shift *= 2
    return x


def _split_bf16(x):
    pieces = []
    for _ in range(N_SPLIT):
        p = x.astype(BF16).astype(F32)
        pieces.append(p)
        x = x - p
    return pieces


def _proj_kernel(*refs, widths, fox_aug, stream_rows, n_aliased):
    x_ref, w_ref, wf_ref, bf_ref = refs[:4]
    if fox_aug:
        (a2_ref, bq_ref, bk_ref, bv_ref, lf_ref,
         qa_ref, ka_ref, vt_ref, k5_ref, v5_ref, bk5_ref, bv5_ref, f_sc) = refs[4 + n_aliased:]
    else:
        (a2_ref, bq_ref, bk_ref, bv_ref, lf_ref, cq_ref, ck_ref, cv_ref,
         bk5_ref, bv5_ref, ck5_ref, cv5_ref) = refs[4 + n_aliased:]
    xb = x_ref[...].astype(BF16)
    rows = xb.shape[0]
    offs = np.concatenate([[0], np.cumsum(widths)])

    def group(g):
        return _dot_nt(xb, w_ref[int(offs[g]):int(offs[g + 1]), :])

    logf = _log_sigmoid(_dot_nt(xb, wf_ref[...]) + bf_ref[...])
    lf_ref[...] = logf
    cq, ck, cv = group(4), group(5), group(6)
    bq, bk, bv = group(1), group(2), group(3)
    a2_ref[...] = group(0)
    if not fox_aug:
        bq_ref[...] = bq
        bk_ref[...] = bk
        bv_ref[...] = bv
        cq_ref[...] = cq
        ck_ref[...] = ck
        cv_ref[...] = cv
        for dst_ref, val in ((bk5_ref, bk), (bv5_ref, bv), (ck5_ref, ck), (cv5_ref, cv)):
            for b in range(rows // stream_rows):
                for h in range(N_HEADS):
                    dst_ref[b, _head_rows(h, stream_rows), :] = _head(val[b * stream_rows:(b + 1) * stream_rows], h)
        return

    bq_ref[...] = (bq * (SCALE * LOG2E)).astype(BF16)
    bk_ref[...] = bk.astype(BF16)
    bv_ref[...] = bv.astype(BF16)
    blocks = stream_rows // rows
    block = pl.program_id(0) % blocks

    @pl.when(block >= blocks - min(BAND, stream_rows) // rows)
    def _():
        for h in range(N_HEADS):
            bk5_ref[_head_rows(h, rows), :] = _head(bk, h)
            bv5_ref[_head_rows(h, rows), :] = _head(bv, h)

    @pl.when(block == 0)
    def _():
        f_sc[...] = jnp.zeros(f_sc.shape, F32)

    cum = _cumsum_rows(logf) + f_sc[...]
    f_sc[...] = cum[rows - 1:rows]
    cum2 = cum * LOG2E
    lane = lax.broadcasted_iota(jnp.int32, (rows, HEAD_DIM), 1)
    for h in range(N_HEADS):
        pieces = _split_bf16(jnp.broadcast_to(cum2[:, h:h + 1], (rows, HEAD_DIM)))
        q_extra = jnp.where(lane < 2 * N_SPLIT, 1.0, 0.0)
        k_extra = jnp.where(lane < N_SPLIT, 1.0, 0.0)
        for n, piece in enumerate(pieces):
            q_extra = jnp.where(lane == n, piece, q_extra)
            k_extra = jnp.where(lane == N_SPLIT + n, -piece, k_extra)
        qa_ref[h, :, :HEAD_DIM] = (_head(cq, h) * (SCALE * LOG2E)).astype(BF16)
        qa_ref[h, :, HEAD_DIM:] = q_extra.astype(BF16)
        ka_ref[h, :, :HEAD_DIM] = _head(ck, h).astype(BF16)
        ka_ref[h, :, HEAD_DIM:] = k_extra.astype(BF16)
        vt_ref[h] = _head(cv, h).T.astype(BF16)
        k5_ref[_head_rows(h, rows), :] = _head(ck, h)
        v5_ref[_head_rows(h, rows), :] = _head(cv, h)


def _proj_sample(x2d, w_all, wf_all, layer, bf_pad, widths, kv_all, seq):
    t, d = x2d.shape
    rows = min(PROJ_ROWS, t)
    order = (0, 1, 2, 3, 7, 4, 5, 6)
    final = pl.BlockSpec((None, rows // seq, seq * N_HEADS, HEAD_DIM), lambda i: (layer, i, 0, 0))
    n_in = 4
    return pl.pallas_call(
        functools.partial(_proj_kernel, widths=widths, fox_aug=False, stream_rows=seq, n_aliased=len(kv_all)),
        grid=(t // rows,),
        in_specs=[pl.BlockSpec((rows, d), lambda i: (i, 0)), pl.BlockSpec((sum(widths[:-1]), d), lambda i: (0, layer), pipeline_mode=pl.Buffered(1)),
                  pl.BlockSpec((widths[-1], d), lambda i: (0, layer), pipeline_mode=pl.Buffered(1)), _resident((1, widths[-1]))]
                 + [pl.BlockSpec(memory_space=pl.ANY)] * len(kv_all),
        out_specs=[pl.BlockSpec((rows, widths[g]), lambda i: (i, 0)) for g in order] + [final] * len(kv_all),
        out_shape=[jax.ShapeDtypeStruct((t, widths[g]), F32) for g in order]
                  + [jax.ShapeDtypeStruct(a.shape, a.dtype) for a in kv_all],
        input_output_aliases={n_in + n: len(order) + n for n in range(len(kv_all))},
        compiler_params=_params(("arbitrary",), 48),
        name="proj_sample",
    )(x2d, w_all, wf_all, bf_pad, *kv_all)


def _proj_prompt(x2d, w_all, wf_all, layer, bf_pad, widths, kv_all, seq):
    t, d = x2d.shape
    rows = PROJ_ROWS
    bpb = seq // rows
    first_new = bpb - min(BAND, seq) // rows
    tok = lambda w, dt: (pl.BlockSpec((rows, w), lambda i: (i, 0)), jax.ShapeDtypeStruct((t, w), dt))
    aug = (pl.BlockSpec((N_HEADS, rows, 2 * HEAD_DIM), lambda i: (0, i, 0)),
           jax.ShapeDtypeStruct((N_HEADS, t, 2 * HEAD_DIM), BF16))
    every = pl.BlockSpec((None, None, rows * N_HEADS, HEAD_DIM), lambda i: (layer, i // bpb, i % bpb, 0))
    newest = pl.BlockSpec((None, None, rows * N_HEADS, HEAD_DIM),
                          lambda i: (layer, i // bpb, jnp.maximum(i % bpb - first_new, 0), 0))
    transposed = (pl.BlockSpec((N_HEADS, HEAD_DIM, rows), lambda i: (0, 0, i)),
                  jax.ShapeDtypeStruct((N_HEADS, HEAD_DIM, t), BF16))
    outs = [tok(widths[0], F32), tok(D_ATT, BF16), tok(D_ATT, BF16), tok(D_ATT, BF16), tok(widths[-1], F32),
            aug, aug, transposed] + [(spec, jax.ShapeDtypeStruct(a.shape, a.dtype))
                                     for spec, a in zip((every, every, newest, newest), kv_all)]
    n_in = 4
    return pl.pallas_call(
        functools.partial(_proj_kernel, widths=widths, fox_aug=True, stream_rows=seq, n_aliased=len(kv_all)),
        grid=(t // rows,),
        in_specs=[pl.BlockSpec((rows, d), lambda i: (i, 0)), pl.BlockSpec((sum(widths[:-1]), d), lambda i: (0, layer), pipeline_mode=pl.Buffered(1)),
                  pl.BlockSpec((widths[-1], d), lambda i: (0, layer), pipeline_mode=pl.Buffered(1)), _resident((1, widths[-1]))]
                 + [pl.BlockSpec(memory_space=pl.ANY)] * len(kv_all),
        out_specs=[o[0] for o in outs],
        out_shape=[o[1] for o in outs],
        scratch_shapes=[pltpu.VMEM((1, widths[-1]), F32)],
        input_output_aliases={n_in + n: len(outs) - len(kv_all) + n for n in range(len(kv_all))},
        compiler_params=_params(("arbitrary",), 52),
        name="proj_prompt",
    )(x2d, w_all, wf_all, bf_pad, *kv_all)


def _rglru_block(xa, ag, lru_refs, h_sc, tail_sc, first_block, reset_first):
    wc_ref, bc_ref, wra_ref, bra_ref, wrx_ref, brx_ref, lam_ref = lru_refs
    rows, d_a = xa.shape
    ext = jnp.concatenate([tail_sc[...], xa], axis=0)
    xc = bc_ref[...] + wc_ref[CONV_W - 1:CONV_W, :] * xa
    for j in range(1, CONV_W):
        shifted = pltpu.roll(ext, j, 0)[SUBLANES:SUBLANES + rows]
        xc = xc + wc_ref[CONV_W - 1 - j:CONV_W - j, :] * shifted
    tail_sc[...] = xa[rows - SUBLANES:rows]

    xcb = xc.astype(BF16)

    def gate(w_ref, b_ref):
        parts = [jnp.dot(xcb[:, n * HEAD_DIM:(n + 1) * HEAD_DIM], w_ref[n], preferred_element_type=F32)
                 for n in range(d_a // HEAD_DIM)]
        return jax.nn.sigmoid(jnp.concatenate(parts, axis=1) + b_ref[...])

    r = gate(wra_ref, bra_ref)
    i_gate = gate(wrx_ref, brx_ref)
    log_a = (-LRU_C) * r * _softplus(-lam_ref[...])
    a = jnp.exp(log_a)
    mult = jnp.exp(0.5 * jnp.log(1.0 - a * a))
    row = lax.broadcasted_iota(jnp.int32, (rows, d_a), 0)
    if reset_first:
        mult = jnp.where(jnp.logical_and(row == 0, first_block), 1.0, mult)
    u = mult * (i_gate * xc)

    n_groups = rows // SUBLANES
    a = a.reshape(n_groups, SUBLANES, d_a)
    u = u.reshape(n_groups, SUBLANES, d_a)
    in_group = lax.broadcasted_iota(jnp.int32, a.shape, 1)
    shift = 1
    while shift < SUBLANES:
        a_prev = pltpu.roll(a, shift, 1)
        u_prev = pltpu.roll(u, shift, 1)
        live = in_group >= shift
        u = jnp.where(live, a * u_prev + u, u)
        a = jnp.where(live, a * a_prev, a)
        shift *= 2
    h_last = h_sc[...]
    groups = []
    for g in range(n_groups):
        h_g = u[g] + a[g] * h_last
        h_last = h_g[SUBLANES - 1:SUBLANES]
        groups.append(h_g)
    h = jnp.concatenate(groups, axis=0)
    h_sc[...] = h_last
    return h * _gelu_tanh(ag), h_last


def _rglru_kernel(ax_ref, ag_ref, cs_ref, h0_ref, *rest, rows, reset_first):
    lru_refs, (y_ref, buf_ref, hl_ref, h_sc, tail_sc) = rest[:7], rest[7:]
    t = pl.program_id(1)

    @pl.when(t == 0)
    def _():
        h_sc[...] = h0_ref[0]
        tail_sc[...] = cs_ref[0]

    xa = ax_ref[0]
    y_ref[0], h_last = _rglru_block(xa, ag_ref[0], lru_refs, h_sc, tail_sc, t == 0, reset_first)

    @pl.when(t == pl.num_programs(1) - 1)
    def _():
        buf_ref[0] = xa[rows - SUBLANES:rows]
        hl_ref[0] = h_last


def _rglru(a2, conv_state8, h0, w_conv, b_conv, w_ra, b_ra, w_rx, b_rx, lam, *, reset_first):
    b, t, d2 = a2.shape
    d_a = d2 // 2
    rows = min(LRU_ROWS, t)
    n_blocks = d_a // HEAD_DIM
    row_vec = _resident((1, d_a))
    gate_w = _resident((n_blocks, HEAD_DIM, HEAD_DIM))
    kern = functools.partial(_rglru_kernel, rows=rows, reset_first=reset_first)
    return pl.pallas_call(
        kern,
        grid=(b, t // rows),
        in_specs=[pl.BlockSpec((1, rows, d_a), lambda i, j: (i, j, 0)),
                  pl.BlockSpec((1, rows, d_a), lambda i, j: (i, j, 1)),
                  pl.BlockSpec((1, SUBLANES, d_a), lambda i, j: (i, 0, 0)),
                  pl.BlockSpec((1, 1, d_a), lambda i, j: (i, 0, 0)),
                  _resident((CONV_W, d_a)), row_vec, gate_w, row_vec, gate_w, row_vec, row_vec],
        out_specs=[pl.BlockSpec((1, rows, d_a), lambda i, j: (i, j, 0)),
                   pl.BlockSpec((1, SUBLANES, d_a), lambda i, j: (i, 0, 0)),
                   pl.BlockSpec((1, 1, d_a), lambda i, j: (i, 0, 0))],
        out_shape=[jax.ShapeDtypeStruct((b, t, d_a), F32),
                   jax.ShapeDtypeStruct((b, SUBLANES, d_a), F32),
                   jax.ShapeDtypeStruct((b, 1, d_a), F32)],
        scratch_shapes=[pltpu.VMEM((1, d_a), F32), pltpu.VMEM((SUBLANES, d_a), F32)],
        compiler_params=_params(("arbitrary", "arbitrary"), 48),
        name="rglru_reset" if reset_first else "rglru_carry",
    )(a2, a2, conv_state8, h0, w_conv, b_conv, w_ra, b_ra, w_rx, b_rx, lam)


def _cumsum_kernel(x_ref, off_ref, o_ref):
    x = x_ref[...]
    n = x.shape[1]
    lane = lax.broadcasted_iota(jnp.int32, x.shape, 1)
    shift = 1
    while shift < n:
        prev = pltpu.roll(x, shift, 1)
        x = jnp.where(lane >= shift, x + prev, x)
        shift *= 2
    o_ref[...] = x + off_ref[...]


def _cumsum_lanes(x, off):
    r, n = x.shape
    return pl.pallas_call(
        _cumsum_kernel,
        grid=(1,),
        in_specs=[pl.BlockSpec((r, n), lambda i: (0, 0)), pl.BlockSpec((r, 1), lambda i: (0, 0))],
        out_specs=pl.BlockSpec((r, n), lambda i: (0, 0)),
        out_shape=jax.ShapeDtypeStruct((r, n), F32),
        compiler_params=_params(("arbitrary",), 32),
        name="cumsum_lanes",
    )(x, off)


def _softmax_pv(s, v_h):
    m = jnp.max(s, axis=-1, keepdims=True)
    p = jnp.exp2(s - m)
    den = jnp.sum(p, axis=-1, keepdims=True)
    return jnp.dot(p.astype(BF16), v_h, preferred_element_type=F32) / den


def _flash_step(scores, values, m_sc, l_sc, acc_sc):
    prev = [(m_sc[h], l_sc[h], acc_sc[h]) for h in range(N_HEADS)]
    new = []
    for (m_prev, l_prev, acc_prev), s, v_h in zip(prev, scores, values):
        m_new = jnp.maximum(m_prev, jnp.max(s, axis=-1, keepdims=True))
        alpha = jnp.exp2(m_prev - m_new)
        p = jnp.exp2(s - m_new)
        new.append((m_new, alpha * l_prev + jnp.sum(p, axis=-1, keepdims=True),
                    alpha * acc_prev + jnp.dot(p.astype(BF16), v_h, preferred_element_type=F32)))
    for h, (m_new, l_new, acc_new) in enumerate(new):
        m_sc[h] = m_new
        l_sc[h] = l_new
        acc_sc[h] = acc_new


def _flash_init(m_sc, l_sc, acc_sc):
    m_sc[...] = jnp.full(m_sc.shape, NEG, F32)
    l_sc[...] = jnp.zeros(l_sc.shape, F32)
    acc_sc[...] = jnp.zeros(acc_sc.shape, F32)


def _flash_finish(o_ref, l_sc, acc_sc):
    for h in range(N_HEADS):
        o_ref[0, :, h * HEAD_DIM:(h + 1) * HEAD_DIM] = acc_sc[h] / l_sc[h]


def _flash_scratch(rows):
    return [pltpu.VMEM((N_HEADS, rows, 1), F32), pltpu.VMEM((N_HEADS, rows, 1), F32),
            pltpu.VMEM((N_HEADS, rows, HEAD_DIM), F32)]


def _rel_row(rel_table):
    near = rel_table[:, 2 * REL_CLIP:]
    far = rel_table[:, :1]
    n_far = REL_ROW // 2 - BAND - REL_CLIP - 1
    row = jnp.concatenate([jnp.broadcast_to(near, (N_HEADS, BAND - REL_CLIP)), rel_table[:, ::-1],
                           jnp.broadcast_to(far, (N_HEADS, n_far)),
                           jnp.broadcast_to(near, (N_HEADS, REL_ROW // 2))], axis=1)
    return row.reshape(N_HEADS, 1, REL_ROW).astype(F32)


def _rel_tile(row_ref, h, rows, cols):
    tile = pltpu.roll(jnp.broadcast_to(row_ref[h], (rows, REL_ROW)), 0, 1, stride=1, stride_axis=0)
    return tile[:, :cols] * LOG2E


def _band_prompt_kernel(*refs, rows, n_prev):
    n_kv = n_prev + 1
    q_ref, k_refs, v_refs = refs[0], refs[1:1 + n_kv], refs[1 + n_kv:1 + 2 * n_kv]
    row_ref, o_ref, bias_sc = refs[1 + 2 * n_kv:]
    cols = n_kv * rows
    i = pl.program_id(1)

    @pl.when(jnp.logical_and(pl.program_id(0) == 0, i == 0))
    def _():
        qc = lax.broadcasted_iota(jnp.int32, (rows, cols), 0) // CHUNK
        kc = lax.broadcasted_iota(jnp.int32, (rows, cols), 1) // CHUNK - (n_prev * rows) // CHUNK
        visible = jnp.logical_and(kc <= qc, kc >= qc - N_PREV_CHUNKS)
        for h in range(N_HEADS):
            bias_sc[h] = jnp.where(visible, _rel_tile(row_ref, h, rows, cols), NEG)

    def attend(clip_start):
        q = q_ref[0]
        k = jnp.concatenate([r[0] for r in k_refs], axis=0)
        v = jnp.concatenate([r[0] for r in v_refs], axis=0)
        for h in range(N_HEADS):
            s = _dot_nt(_head(q, h), _head(k, h)) + bias_sc[h]
            if clip_start:
                col = lax.broadcasted_iota(jnp.int32, (rows, cols), 1)
                s = jnp.where(col >= (n_prev - i) * rows, s, NEG)
            o_ref[0, :, h * HEAD_DIM:(h + 1) * HEAD_DIM] = _softmax_pv(s, _head(v, h))

    @pl.when(i < n_prev)
    def _():
        attend(True)

    @pl.when(i >= n_prev)
    def _():
        attend(False)


def _band_prompt(q, k, v, rel_row):
    b, s, d = q.shape
    rows = BAND_ROWS
    n_prev = BAND // rows
    assert BAND == n_prev * rows and s % rows == 0 and rows + BAND <= REL_ROW // 2
    back = lambda steps: pl.BlockSpec((1, rows, d), lambda i, j: (i, jnp.maximum(j - steps, 0), 0))
    window = [back(steps) for steps in range(n_prev, -1, -1)]
    return pl.pallas_call(
        functools.partial(_band_prompt_kernel, rows=rows, n_prev=n_prev),
        grid=(b, s // rows),
        in_specs=[back(0)] + window + window + [_resident(rel_row.shape)],
        out_specs=back(0),
        out_shape=jax.ShapeDtypeStruct((b, s, d), F32),
        scratch_shapes=[pltpu.VMEM((N_HEADS, rows, (n_prev + 1) * rows), F32)],
        compiler_params=_params(("arbitrary", "arbitrary"), 48),
        name="band_prompt",
    )(q, *[k] * (n_prev + 1), *[v] * (n_prev + 1), rel_row)


def _band_sample_kernel(q_ref, kn_ref, vn_ref, kc_ref, vc_ref, row_ref, o_ref, bias_sc, *, t_new, n_cache):
    @pl.when(pl.program_id(0) == 0)
    def _():
        for h in range(N_HEADS):
            bias_sc[h] = _rel_tile(row_ref, h, t_new, n_cache + t_new)

    q = q_ref[0]
    kn = kn_ref[0].astype(BF16)
    vn = vn_ref[0].astype(BF16)
    for h in range(N_HEADS):
        cached = _head_rows(h, n_cache)
        k = jnp.concatenate([kc_ref[cached, :].astype(BF16), _head(kn, h)], axis=0)
        v = jnp.concatenate([vc_ref[cached, :].astype(BF16), _head(vn, h)], axis=0)
        s = _dot_nt((_head(q, h) * (SCALE * LOG2E)).astype(BF16), k) + bias_sc[h]
        o_ref[0, :, h * HEAD_DIM:(h + 1) * HEAD_DIM] = _softmax_pv(s, v)


def _band_sample(q, k_new, v_new, k_cache, v_cache, layer, rel_row):
    b, t, d = q.shape
    n_c = k_cache.shape[2] // N_HEADS
    assert n_c == BAND
    new = pl.BlockSpec((1, t, d), lambda i: (i, 0, 0))
    old = pl.BlockSpec((None, None, n_c * N_HEADS, HEAD_DIM), lambda i: (layer, i, 0, 0))
    return pl.pallas_call(
        functools.partial(_band_sample_kernel, t_new=t, n_cache=n_c),
        grid=(b,),
        in_specs=[new, new, new, old, old, _resident(rel_row.shape)],
        out_specs=new,
        out_shape=jax.ShapeDtypeStruct((b, t, d), F32),
        scratch_shapes=[pltpu.VMEM((N_HEADS, t, n_c + t), F32)],
        compiler_params=_params(("arbitrary",), 32),
        name="band_sample",
    )(q, k_new, v_new, k_cache, v_cache, rel_row)


def _fox_prompt_kernel(qi_ref, kj_ref, qa_ref, ka_ref, vt_ref, o_ref, m_sc, l_sc, acc_sc, *, rows):
    i = qi_ref[pl.program_id(1)]
    j = kj_ref[pl.program_id(1)]

    @pl.when(j == 0)
    def _():
        _flash_init(m_sc, l_sc, acc_sc)

    def attend(diagonal):
        scores = [_dot_nt(ka_ref[h], qa_ref[h]) for h in range(N_HEADS)]
        if diagonal:
            causal = (lax.broadcasted_iota(jnp.int32, (rows, rows), 0)
                      <= lax.broadcasted_iota(jnp.int32, (rows, rows), 1))
            scores = [jnp.where(causal, s, NEG) for s in scores]
        prev = [(m_sc[h], l_sc[h], acc_sc[h]) for h in range(N_HEADS)]
        new = []
        for h, ((m_prev, l_prev, acc_prev), s) in enumerate(zip(prev, scores)):
            m_new = jnp.maximum(m_prev, jnp.max(s, axis=0, keepdims=True))
            alpha = jnp.exp2(m_prev - m_new)
            p = jnp.exp2(s - m_new)
            new.append((m_new, alpha * l_prev + jnp.sum(p, axis=0, keepdims=True),
                        alpha * acc_prev + jnp.dot(vt_ref[h], p.astype(BF16), preferred_element_type=F32)))
        for h, (m_new, l_new, acc_new) in enumerate(new):
            m_sc[h] = m_new
            l_sc[h] = l_new
            acc_sc[h] = acc_new

    @pl.when(j < i)
    def _():
        attend(False)

    @pl.when(j == i)
    def _():
        attend(True)
        for h in range(N_HEADS):
            o_ref[0, :, h * HEAD_DIM:(h + 1) * HEAD_DIM] = (acc_sc[h] / l_sc[h]).T


def _fox_prompt(qa, ka, vt, batch):
    _, t, da = qa.shape
    rows = FOX_ROWS
    n = t // batch // rows
    pairs = [(i, j) for i in range(n) for j in range(i + 1)]
    qi = jnp.asarray([p[0] for p in pairs], jnp.int32)
    kj = jnp.asarray([p[1] for p in pairs], jnp.int32)
    grid_spec = pltpu.PrefetchScalarGridSpec(
        num_scalar_prefetch=2,
        grid=(batch, len(pairs)),
        in_specs=[pl.BlockSpec((N_HEADS, rows, da), lambda b, p, qi, kj: (0, b * n + qi[p], 0)),
                  pl.BlockSpec((N_HEADS, rows, da), lambda b, p, qi, kj: (0, b * n + kj[p], 0)),
                  pl.BlockSpec((N_HEADS, HEAD_DIM, rows), lambda b, p, qi, kj: (0, 0, b * n + kj[p]))],
        out_specs=pl.BlockSpec((1, rows, D_ATT), lambda b, p, qi, kj: (b, qi[p], 0)),
        scratch_shapes=[pltpu.VMEM((N_HEADS, 1, rows), F32), pltpu.VMEM((N_HEADS, 1, rows), F32),
                        pltpu.VMEM((N_HEADS, HEAD_DIM, rows), F32)])
    return pl.pallas_call(
        functools.partial(_fox_prompt_kernel, rows=rows),
        grid_spec=grid_spec,
        out_shape=jax.ShapeDtypeStruct((batch, t // batch, D_ATT), F32),
        compiler_params=_params(("arbitrary", "arbitrary"), 48),
        name="fox_prompt",
    )(qi, kj, qa, ka, vt)


def _fox_sample_kernel(*refs, t_new, n_parts):
    q_ref, kc_refs, vc_refs = refs[0], refs[1:1 + n_parts], refs[1 + n_parts:1 + 2 * n_parts]
    kn_ref, vn_ref, fq_ref, fkc_ref, fkn_ref, o_ref, m_sc, l_sc, acc_sc = refs[1 + 2 * n_parts:]
    j = pl.program_id(1)

    @pl.when(j == 0)
    def _():
        _flash_init(m_sc, l_sc, acc_sc)

    q = q_ref[0]
    qs = [(_head(q, h) * (SCALE * LOG2E)).astype(BF16) for h in range(N_HEADS)]
    fq = fq_ref[0] * LOG2E

    def bias(fk, h):
        return fq[:, h:h + 1] - fk[h:h + 1, :] * LOG2E

    fkc = fkc_ref[0]
    part_rows = fkc.shape[1] // n_parts

    def cached(part_refs, h):
        rows_h = _head_rows(h, part_rows)
        return jnp.concatenate([r[rows_h, :] for r in part_refs], axis=0).astype(BF16)

    _flash_step([_dot_nt(qs[h], cached(kc_refs, h)) + bias(fkc, h) for h in range(N_HEADS)],
                [cached(vc_refs, h) for h in range(N_HEADS)], m_sc, l_sc, acc_sc)

    @pl.when(j == pl.num_programs(1) - 1)
    def _():
        kn = kn_ref[0].astype(BF16)
        vn = vn_ref[0].astype(BF16)
        fkn = fkn_ref[0]
        causal = (lax.broadcasted_iota(jnp.int32, (t_new, t_new), 1)
                  <= lax.broadcasted_iota(jnp.int32, (t_new, t_new), 0))
        _flash_step([jnp.where(causal, _dot_nt(qs[h], _head(kn, h)) + bias(fkn, h), NEG) for h in range(N_HEADS)],
                    [_head(vn, h) for h in range(N_HEADS)], m_sc, l_sc, acc_sc)
        _flash_finish(o_ref, l_sc, acc_sc)


def _fox_sample(q, k_new, v_new, k_cache, v_cache, layer, fq, fk_cache, fk_new):
    b, t, d = q.shape
    p = k_cache.shape[2] // N_HEADS
    rows = min(FOX_CACHE_ROWS, p)
    new = pl.BlockSpec((1, t, d), lambda i, j: (i, 0, 0))
    n_parts = FOX_CACHE_PARTS
    part = lambda n: pl.BlockSpec((None, None, rows // n_parts * N_HEADS, HEAD_DIM),
                                  lambda i, j: (layer, i, j * n_parts + n, 0))
    old = [part(n) for n in range(n_parts)]
    return pl.pallas_call(
        functools.partial(_fox_sample_kernel, t_new=t, n_parts=n_parts),
        grid=(b, p // rows),
        in_specs=[new, *old, *old, new, new,
                  pl.BlockSpec((1, t, N_HEADS), lambda i, j: (i, 0, 0)),
                  pl.BlockSpec((1, N_HEADS, rows), lambda i, j: (i, 0, j)),
                  pl.BlockSpec((1, N_HEADS, t), lambda i, j: (i, 0, 0))],
        out_specs=new,
        out_shape=jax.ShapeDtypeStruct((b, t, d), F32),
        scratch_shapes=_flash_scratch(t),
        compiler_params=_params(("arbitrary", "arbitrary"), 48),
        name="fox_sample",
    )(q, *[k_cache] * n_parts, *[v_cache] * n_parts, k_new, v_new, fq, fk_cache, fk_new)


def _mix_kernel(x_ref, ya_ref, yb_ref, yc_ref, g_ref, w_ref, lg_ref, lb_ref, o_ref, *, alpha):
    d_a = ya_ref.shape[-1]
    d_b = yb_ref.shape[-1]
    g = g_ref[...]
    y = jnp.concatenate([_rms_norm(ya_ref[...], g[:, :d_a]),
                         _rms_norm(yb_ref[...], g[:, d_a:d_a + d_b]),
                         _rms_norm(yc_ref[...], g[:, d_a + d_b:])], axis=1)
    z = alpha * x_ref[...] + jnp.dot(y.astype(BF16), w_ref[...], preferred_element_type=F32)
    o_ref[...] = _layer_norm(z, lg_ref[...], lb_ref[...])


def _mix(x2d, ya, yb, yc, g_mix, w_out_all, layer, ln_g, ln_b, alpha):
    t, d = x2d.shape
    rows = min(MIX_ROWS, t)
    tok = lambda w: pl.BlockSpec((rows, w), lambda i: (i, 0))
    vec = _resident((1, d))
    return pl.pallas_call(
        functools.partial(_mix_kernel, alpha=alpha),
        grid=(t // rows,),
        in_specs=[tok(d), tok(ya.shape[1]), tok(yb.shape[1]), tok(yc.shape[1]), vec,
                  _layer_block(layer, w_out_all.shape[1:]), vec, vec],
        out_specs=tok(d),
        out_shape=jax.ShapeDtypeStruct((t, d), F32),
        compiler_params=_params(("arbitrary",), 48),
        name="mix",
    )(x2d, ya, yb, yc, g_mix, w_out_all, ln_g, ln_b)


def _ffn_kernel(x_ref, wu_ref, wd_ref, lg_ref, lb_ref, o_ref, xb_sc, *, alpha):
    j = pl.program_id(1)
    last = pl.num_programs(1) - 1
    rows = o_ref.shape[0]
    part = rows // FFN_EDGE_PARTS
    parts = [slice(r * part, (r + 1) * part) for r in range(FFN_EDGE_PARTS)]

    def term(sl):
        hid = jnp.maximum(jnp.dot(xb_sc[sl, :], wu_ref[...], preferred_element_type=F32), 0.0)
        return jnp.dot((hid * hid).astype(BF16), wd_ref[...], preferred_element_type=F32)

    @pl.when(j == 0)
    def _():
        for sl in parts:
            xb_sc[sl, :] = x_ref[sl, :].astype(BF16)
            o_ref[sl, :] = term(sl)

    @pl.when(jnp.logical_and(j > 0, j < last))
    def _():
        o_ref[...] += term(slice(None))

    @pl.when(j == last)
    def _():
        for sl in parts:
            o_ref[sl, :] = _layer_norm(alpha * x_ref[sl, :] + (o_ref[sl, :] + term(sl)), lg_ref[...], lb_ref[...])


def _ffn(x2d, w_up_all, w_down_all, layer, ln_g, ln_b, alpha):
    t, d = x2d.shape
    d_ff = w_up_all.shape[2]
    rows, cols = min(FFN_ROWS, t), FFN_COLS
    vec = _resident((1, d))
    return pl.pallas_call(
        functools.partial(_ffn_kernel, alpha=alpha),
        grid=(t // rows, d_ff // cols),
        in_specs=[pl.BlockSpec((rows, d), lambda i, j: (i, 0)),
                  pl.BlockSpec((None, d, cols), lambda i, j: (layer, 0, j)),
                  pl.BlockSpec((None, cols, d), lambda i, j: (layer, j, 0)),
                  vec, vec],
        out_specs=pl.BlockSpec((rows, d), lambda i, j: (i, 0)),
        out_shape=jax.ShapeDtypeStruct((t, d), F32),
        scratch_shapes=[pltpu.VMEM((rows, d), BF16)],
        compiler_params=_params(("arbitrary", "arbitrary"), 56),
        name="ffn",
    )(x2d, w_up_all, w_down_all, ln_g, ln_b)


def _cum_logf(logf_t, off):
    b, h, t = logf_t.shape
    t_pad = -(-t // LANES) * LANES
    x = logf_t.reshape(b * h, t)
    if t_pad != t:
        x = jnp.pad(x, ((0, 0), (0, t_pad - t)))
    return _cumsum_lanes(x, off.reshape(b * h, 1))[:, :t].reshape(b, h, t)


def _row(v):
    return v.reshape(1, -1)


def kernel(x_prompt, x_sample, state_conv, state_lru, cache_band_k, cache_band_v, cache_fox_k, cache_fox_v,
           cache_fox_logf, w_in, w_conv, b_conv, w_rg_a, b_rg_a, w_rg_x, b_rg_x, lru_lambda, rel_bias,
           b_forget, g_mix, w_out, ln1_g, ln1_b, w_up, w_down, ln2_g, ln2_b):
    depth = w_in.shape[0]
    bp, s, d = x_prompt.shape
    bs, ts, _ = x_sample.shape
    d_a = w_conv.shape[-1]
    alpha = float((2 * depth) ** 0.25)
    d_main = 2 * d_a + 6 * D_ATT
    widths = (2 * d_a,) + (D_ATT,) * 6 + (LANES,)
    n_band_p = min(BAND, s)
    p_len = cache_fox_k.shape[2]

    w_in_t = jnp.transpose(w_in, (2, 0, 1))
    w_in_b = w_in_t[:d_main].reshape(d_main, depth * d).astype(BF16)
    w_f_b = jnp.pad(w_in_t[d_main:].reshape(N_HEADS, depth * d),
                    ((0, LANES - N_HEADS), (0, 0))).astype(BF16)
    w_ra_b, w_rx_b = w_rg_a.astype(BF16), w_rg_x.astype(BF16)
    w_out_b, w_up_b, w_down_b = w_out.astype(BF16), w_up.astype(BF16), w_down.astype(BF16)

    xp = x_prompt.reshape(bp * s, d)
    xs = x_sample.reshape(bs * ts, d)
    outs = [[] for _ in range(6)]
    p_kv = [jnp.zeros((depth, bp, rows_ * N_HEADS, HEAD_DIM), F32) for rows_ in (s, s, n_band_p, n_band_p)]
    s_kv = [jnp.zeros((depth, bs, ts * N_HEADS, HEAD_DIM), F32) for _ in range(4)]
    band_kc, band_vc = _interleaved(cache_band_k), _interleaved(cache_band_v)
    fox_kc, fox_vc = _interleaved(cache_fox_k), _interleaved(cache_fox_v)

    for l in range(depth):
        bf_pad = jnp.pad(b_forget[l], (0, LANES - N_HEADS)).reshape(1, LANES)
        lru_w = (w_conv[l], _row(b_conv[l]), w_ra_b[l], _row(b_rg_a[l]), w_rx_b[l], _row(b_rg_x[l]),
                 _row(lru_lambda[l]))
        rel_row = _rel_row(rel_bias[l])

        def finish(x2d, ya, yb, yc):
            x1 = _mix(x2d, ya, yb, yc, _row(g_mix[l]), w_out_b, l, _row(ln1_g[l]), _row(ln1_b[l]), alpha)
            return _ffn(x1, w_up_b, w_down_b, l, _row(ln2_g[l]), _row(ln2_b[l]), alpha)

        a2, bq, bk, bv, lf, qa, ka, vt, *p_kv = _proj_prompt(xp, w_in_b, w_f_b, l, bf_pad, widths, p_kv, s)
        ya, buf, h_last = _rglru(a2.reshape(bp, s, 2 * d_a), jnp.zeros((bp, SUBLANES, d_a), F32),
                                 jnp.zeros((bp, 1, d_a), F32), *lru_w, reset_first=True)
        r3 = lambda z: z.reshape(bp, s, D_ATT)
        yb = _band_prompt(r3(bq), r3(bk), r3(bv), rel_row)
        yc = _fox_prompt(qa, ka, vt, bp)
        xp = finish(xp, ya.reshape(bp * s, d_a), yb.reshape(bp * s, D_ATT), yc.reshape(bp * s, D_ATT))
        for dst, val in zip(outs[:3], (buf[:, SUBLANES - (CONV_W - 1):], h_last.reshape(bp, d_a),
                                       lf[:, :N_HEADS].reshape(bp, s, N_HEADS))):
            dst.append(val)

        a2, bq, bk, bv, lf, cq, ck, cv, *s_kv = _proj_sample(xs, w_in_b, w_f_b, l, bf_pad, widths, s_kv, ts)
        conv8 = jnp.pad(state_conv[l], ((0, 0), (SUBLANES - (CONV_W - 1), 0), (0, 0)))
        ya, buf, h_last = _rglru(a2.reshape(bs, ts, 2 * d_a), conv8, state_lru[l].reshape(bs, 1, d_a),
                                 *lru_w, reset_first=False)
        r3 = lambda z: z.reshape(bs, ts, D_ATT)
        yb = _band_sample(r3(bq), r3(bk), r3(bv), band_kc, band_vc, l, rel_row)
        logf = lf[:, :N_HEADS].reshape(bs, ts, N_HEADS)
        fk_cache = _cum_logf(jnp.swapaxes(cache_fox_logf[l], 1, 2), jnp.zeros((bs, N_HEADS, 1), F32))
        fk_new = _cum_logf(jnp.swapaxes(logf, 1, 2), fk_cache[:, :, p_len - 1:])
        yc = _fox_sample(r3(cq), r3(ck), r3(cv), fox_kc, fox_vc, l,
                         jnp.swapaxes(fk_new, 1, 2), fk_cache, fk_new)
        xs = finish(xs, ya.reshape(bs * ts, d_a), yb.reshape(bs * ts, D_ATT), yc.reshape(bs * ts, D_ATT))
        for dst, val in zip(outs[3:], (buf[:, SUBLANES - (CONV_W - 1):], h_last.reshape(bs, d_a), logf)):
            dst.append(val)

    st = [jnp.stack(o, axis=0) for o in outs]
    heads = lambda z: z.reshape(z.shape[:2] + (z.shape[2] // N_HEADS, N_HEADS, HEAD_DIM))
    fox_k, fox_v, band_k, band_v = [heads(z) for z in p_kv]
    return (xp.reshape(bp, s, d), xs.reshape(bs, ts, d),
            st[0], st[1], band_k, band_v, fox_k, fox_v, st[2],
            st[3], st[4], *[heads(z) for z in s_kv], st[5])
```

```python
import functools

import numpy as np
import jax
import jax.numpy as jnp
from jax import lax
from jax.experimental import pallas as pl
from jax.experimental.pallas import tpu as pltpu

F32 = jnp.float32
BF16 = jnp.bfloat16

HEAD_DIM = 128
CHUNK = 64
N_PREV_CHUNKS = 8
BAND = N_PREV_CHUNKS * CHUNK
REL_CLIP = 256
CONV_W = 4
LRU_C = 8.0
LN_EPS = 1e-5
NEG = -1e30
SCALE = HEAD_DIM ** -0.5
LOG2E = float(np.log2(np.e))
N_HEADS = 4
D_ATT = N_HEADS * HEAD_DIM
LANES = 128
SUBLANES = 8
MIB = 1 << 20
N_SPLIT = 3
REL_ROW = 4 * BAND

PROJ_ROWS = 256
LRU_ROWS = 256
BAND_ROWS = 256
FOX_ROWS = 512
FOX_CACHE_ROWS = 2048
FOX_CACHE_PARTS = 1
MIX_ROWS = 512
FFN_ROWS = 512
FFN_COLS = 2048
FFN_CHUNK = 1024
FFN_EDGE_PARTS = 4


def _params(semantics, vmem_mib):
    return pltpu.CompilerParams(dimension_semantics=semantics, vmem_limit_bytes=vmem_mib * MIB)


def _resident(shape):
    return pl.BlockSpec(shape, lambda *_: (0,) * len(shape), pipeline_mode=pl.Buffered(1))


def _layer_block(layer, shape):
    return pl.BlockSpec((None,) + tuple(shape), lambda *_: (layer,) + (0,) * len(shape),
                        pipeline_mode=pl.Buffered(1))


def _log_sigmoid(z):
    return jnp.minimum(z, 0.0) - jnp.log1p(jnp.exp(-jnp.abs(z)))


def _softplus(z):
    return jnp.maximum(z, 0.0) + jnp.log1p(jnp.exp(-jnp.abs(z)))


def _gelu_tanh(x):
    c = np.sqrt(2.0 / np.pi).astype(np.float32)
    return 0.5 * x * (1.0 + jnp.tanh(c * (x + 0.044715 * (x * x * x))))


def _layer_norm(z, g, b):
    mu = jnp.mean(z, axis=-1, keepdims=True)
    zc = z - mu
    var = jnp.mean(zc * zc, axis=-1, keepdims=True)
    return zc * lax.rsqrt(var + LN_EPS) * g + b


def _rms_norm(y, g):
    return y * lax.rsqrt(jnp.mean(y * y, axis=-1, keepdims=True) + LN_EPS) * g


def _dot_nt(a, b):
    return lax.dot_general(a, b, (((1,), (1,)), ((), ())), preferred_element_type=F32)


def _head(x, h):
    return x[:, h * HEAD_DIM:(h + 1) * HEAD_DIM]


def _head_rows(h, rows):
    return pl.ds(h, rows, stride=N_HEADS)


def _interleaved(a):
    return a.reshape(a.shape[:-3] + (a.shape[-3] * N_HEADS, HEAD_DIM))


def _cumsum_rows(x):
    row = lax.broadcasted_iota(jnp.int32, x.shape, 0)
    shift = 1
    while shift < x.shape[0]:
        x = jnp.where(row >= shift, x + pltpu.roll(x, shift, 0), x)
        shift *= 2
    return x


def _split_bf16(x):
    pieces = []
    for _ in range(N_SPLIT):
        p = x.astype(BF16).astype(F32)
        pieces.append(p)
        x = x - p
    return pieces


def _proj_kernel(*refs, widths, fox_aug, stream_rows, n_aliased):
    x_ref, w_ref, wf_ref, bf_ref = refs[:4]
    if fox_aug:
        (a2_ref, bq_ref, bk_ref, bv_ref, lf_ref,
         qa_ref, ka_ref, vt_ref, k5_ref, v5_ref, bk5_ref, bv5_ref, f_sc) = refs[4 + n_aliased:]
    else:
        (a2_ref, bq_ref, bk_ref, bv_ref, lf_ref, cq_ref, ck_ref, cv_ref,
         bk5_ref, bv5_ref, ck5_ref, cv5_ref) = refs[4 + n_aliased:]
    xb = x_ref[...].astype(BF16)
    rows = xb.shape[0]
    offs = np.concatenate([[0], np.cumsum(widths)])

    def group(g):
        return _dot_nt(xb, w_ref[int(offs[g]):int(offs[g + 1]), :])

    logf = _log_sigmoid(_dot_nt(xb, wf_ref[...]) + bf_ref[...])
    lf_ref[...] = logf
    cq, ck, cv = group(4), group(5), group(6)
    bq, bk, bv = group(1), group(2), group(3)
    a2_ref[...] = group(0)
    if not fox_aug:
        bq_ref[...] = bq
        bk_ref[...] = bk
        bv_ref[...] = bv
        cq_ref[...] = cq
        ck_ref[...] = ck
        cv_ref[...] = cv
        for dst_ref, val in ((bk5_ref, bk), (bv5_ref, bv), (ck5_ref, ck), (cv5_ref, cv)):
            for b in range(rows // stream_rows):
                for h in range(N_HEADS):
                    dst_ref[b, _head_rows(h, stream_rows), :] = _head(val[b * stream_rows:(b + 1) * stream_rows], h)
        return

    bq_ref[...] = (bq * (SCALE * LOG2E)).astype(BF16)
    bk_ref[...] = bk.astype(BF16)
    bv_ref[...] = bv.astype(BF16)
    blocks = stream_rows // rows
    block = pl.program_id(0) % blocks

    @pl.when(block >= blocks - min(BAND, stream_rows) // rows)
    def _():
        for h in range(N_HEADS):
            bk5_ref[_head_rows(h, rows), :] = _head(bk, h)
            bv5_ref[_head_rows(h, rows), :] = _head(bv, h)

    @pl.when(block == 0)
    def _():
        f_sc[...] = jnp.zeros(f_sc.shape, F32)

    cum = _cumsum_rows(logf) + f_sc[...]
    f_sc[...] = cum[rows - 1:rows]
    cum2 = cum * LOG2E
    lane = lax.broadcasted_iota(jnp.int32, (rows, HEAD_DIM), 1)
    for h in range(N_HEADS):
        pieces = _split_bf16(jnp.broadcast_to(cum2[:, h:h + 1], (rows, HEAD_DIM)))
        q_extra = jnp.where(lane < 2 * N_SPLIT, 1.0, 0.0)
        k_extra = jnp.where(lane < N_SPLIT, 1.0, 0.0)
        for n, piece in enumerate(pieces):
            q_extra = jnp.where(lane == n, piece, q_extra)
            k_extra = jnp.where(lane == N_SPLIT + n, -piece, k_extra)
        qa_ref[h, :, :HEAD_DIM] = (_head(cq, h) * (SCALE * LOG2E)).astype(BF16)
        qa_ref[h, :, HEAD_DIM:] = q_extra.astype(BF16)
        ka_ref[h, :, :HEAD_DIM] = _head(ck, h).astype(BF16)
        ka_ref[h, :, HEAD_DIM:] = k_extra.astype(BF16)
        vt_ref[h] = _head(cv, h).T.astype(BF16)
        k5_ref[_head_rows(h, rows), :] = _head(ck, h)
        v5_ref[_head_rows(h, rows), :] = _head(cv, h)


def _proj_sample(x2d, w_all, wf_all, layer, bf_pad, widths, kv_all, seq):
    t, d = x2d.shape
    rows = min(PROJ_ROWS, t)
    order = (0, 1, 2, 3, 7, 4, 5, 6)
    final = pl.BlockSpec((None, rows // seq, seq * N_HEADS, HEAD_DIM), lambda i: (layer, i, 0, 0))
    n_in = 4
    return pl.pallas_call(
        functools.partial(_proj_kernel, widths=widths, fox_aug=False, stream_rows=seq, n_aliased=len(kv_all)),
        grid=(t // rows,),
        in_specs=[pl.BlockSpec((rows, d), lambda i: (i, 0)), pl.BlockSpec((sum(widths[:-1]), d), lambda i: (0, layer), pipeline_mode=pl.Buffered(1)),
                  pl.BlockSpec((widths[-1], d), lambda i: (0, layer), pipeline_mode=pl.Buffered(1)), _resident((1, widths[-1]))]
                 + [pl.BlockSpec(memory_space=pl.ANY)] * len(kv_all),
        out_specs=[pl.BlockSpec((rows, widths[g]), lambda i: (i, 0)) for g in order] + [final] * len(kv_all),
        out_shape=[jax.ShapeDtypeStruct((t, widths[g]), F32) for g in order]
                  + [jax.ShapeDtypeStruct(a.shape, a.dtype) for a in kv_all],
        input_output_aliases={n_in + n: len(order) + n for n in range(len(kv_all))},
        compiler_params=_params(("arbitrary",), 48),
        name="proj_sample",
    )(x2d, w_all, wf_all, bf_pad, *kv_all)


def _proj_prompt(x2d, w_all, wf_all, layer, bf_pad, widths, kv_all, seq):
    t, d = x2d.shape
    rows = PROJ_ROWS
    bpb = seq // rows
    first_new = bpb - min(BAND, seq) // rows
    tok = lambda w, dt: (pl.BlockSpec((rows, w), lambda i: (i, 0)), jax.ShapeDtypeStruct((t, w), dt))
    aug = (pl.BlockSpec((N_HEADS, rows, 2 * HEAD_DIM), lambda i: (0, i, 0)),
           jax.ShapeDtypeStruct((N_HEADS, t, 2 * HEAD_DIM), BF16))
    every = pl.BlockSpec((None, None, rows * N_HEADS, HEAD_DIM), lambda i: (layer, i // bpb, i % bpb, 0))
    newest = pl.BlockSpec((None, None, rows * N_HEADS, HEAD_DIM),
                          lambda i: (layer, i // bpb, jnp.maximum(i % bpb - first_new, 0), 0))
    transposed = (pl.BlockSpec((N_HEADS, HEAD_DIM, rows), lambda i: (0, 0, i)),
                  jax.ShapeDtypeStruct((N_HEADS, HEAD_DIM, t), BF16))
    outs = [tok(widths[0], F32), tok(D_ATT, BF16), tok(D_ATT, BF16), tok(D_ATT, BF16), tok(widths[-1], F32),
            aug, aug, transposed] + [(spec, jax.ShapeDtypeStruct(a.shape, a.dtype))
                                     for spec, a in zip((every, every, newest, newest), kv_all)]
    n_in = 4
    return pl.pallas_call(
        functools.partial(_proj_kernel, widths=widths, fox_aug=True, stream_rows=seq, n_aliased=len(kv_all)),
        grid=(t // rows,),
        in_specs=[pl.BlockSpec((rows, d), lambda i: (i, 0)), pl.BlockSpec((sum(widths[:-1]), d), lambda i: (0, layer), pipeline_mode=pl.Buffered(1)),
                  pl.BlockSpec((widths[-1], d), lambda i: (0, layer), pipeline_mode=pl.Buffered(1)), _resident((1, widths[-1]))]
                 + [pl.BlockSpec(memory_space=pl.ANY)] * len(kv_all),
        out_specs=[o[0] for o in outs],
        out_shape=[o[1] for o in outs],
        scratch_shapes=[pltpu.VMEM((1, widths[-1]), F32)],
        input_output_aliases={n_in + n: len(outs) - len(kv_all) + n for n in range(len(kv_all))},
        compiler_params=_params(("arbitrary",), 52),
        name="proj_prompt",
    )(x2d, w_all, wf_all, bf_pad, *kv_all)


def _rglru_block(xa, ag, lru_refs, h_sc, tail_sc, first_block, reset_first):
    wc_ref, bc_ref, wra_ref, bra_ref, wrx_ref, brx_ref, lam_ref = lru_refs
    rows, d_a = xa.shape
    ext = jnp.concatenate([tail_sc[...], xa], axis=0)
    xc = bc_ref[...] + wc_ref[CONV_W - 1:CONV_W, :] * xa
    for j in range(1, CONV_W):
        shifted = pltpu.roll(ext, j, 0)[SUBLANES:SUBLANES + rows]
        xc = xc + wc_ref[CONV_W - 1 - j:CONV_W - j, :] * shifted
    tail_sc[...] = xa[rows - SUBLANES:rows]

    xcb = xc.astype(BF16)

    def gate(w_ref, b_ref):
        parts = [jnp.dot(xcb[:, n * HEAD_DIM:(n + 1) * HEAD_DIM], w_ref[n], preferred_element_type=F32)
                 for n in range(d_a // HEAD_DIM)]
        return jax.nn.sigmoid(jnp.concatenate(parts, axis=1) + b_ref[...])

    r = gate(wra_ref, bra_ref)
    i_gate = gate(wrx_ref, brx_ref)
    log_a = (-LRU_C) * r * _softplus(-lam_ref[...])
    a = jnp.exp(log_a)
    mult = jnp.exp(0.5 * jnp.log(1.0 - a * a))
    row = lax.broadcasted_iota(jnp.int32, (rows, d_a), 0)
    if reset_first:
        mult = jnp.where(jnp.logical_and(row == 0, first_block), 1.0, mult)
    u = mult * (i_gate * xc)

    n_groups = rows // SUBLANES
    a = a.reshape(n_groups, SUBLANES, d_a)
    u = u.reshape(n_groups, SUBLANES, d_a)
    in_group = lax.broadcasted_iota(jnp.int32, a.shape, 1)
    shift = 1
    while shift < SUBLANES:
        a_prev = pltpu.roll(a, shift, 1)
        u_prev = pltpu.roll(u, shift, 1)
        live = in_group >= shift
        u = jnp.where(live, a * u_prev + u, u)
        a = jnp.where(live, a * a_prev, a)
        shift *= 2
    h_last = h_sc[...]
    groups = []
    for g in range(n_groups):
        h_g = u[g] + a[g] * h_last
        h_last = h_g[SUBLANES - 1:SUBLANES]
        groups.append(h_g)
    h = jnp.concatenate(groups, axis=0)
    h_sc[...] = h_last
    return h * _gelu_tanh(ag), h_last


def _rglru_kernel(ax_ref, ag_ref, cs_ref, h0_ref, *rest, rows, reset_first):
    lru_refs, (y_ref, buf_ref, hl_ref, h_sc, tail_sc) = rest[:7], rest[7:]
    t = pl.program_id(1)

    @pl.when(t == 0)
    def _():
        h_sc[...] = h0_ref[0]
        tail_sc[...] = cs_ref[0]

    xa = ax_ref[0]
    y_ref[0], h_last = _rglru_block(xa, ag_ref[0], lru_refs, h_sc, tail_sc, t == 0, reset_first)

    @pl.when(t == pl.num_programs(1) - 1)
    def _():
        buf_ref[0] = xa[rows - SUBLANES:rows]
        hl_ref[0] = h_last


def _rglru(a2, conv_state8, h0, w_conv, b_conv, w_ra, b_ra, w_rx, b_rx, lam, *, reset_first):
    b, t, d2 = a2.shape
    d_a = d2 // 2
    rows = min(LRU_ROWS, t)
    n_blocks = d_a // HEAD_DIM
    row_vec = _resident((1, d_a))
    gate_w = _resident((n_blocks, HEAD_DIM, HEAD_DIM))
    kern = functools.partial(_rglru_kernel, rows=rows, reset_first=reset_first)
    return pl.pallas_call(
        kern,
        grid=(b, t // rows),
        in_specs=[pl.BlockSpec((1, rows, d_a), lambda i, j: (i, j, 0)),
                  pl.BlockSpec((1, rows, d_a), lambda i, j: (i, j, 1)),
                  pl.BlockSpec((1, SUBLANES, d_a), lambda i, j: (i, 0, 0)),
                  pl.BlockSpec((1, 1, d_a), lambda i, j: (i, 0, 0)),
                  _resident((CONV_W, d_a)), row_vec, gate_w, row_vec, gate_w, row_vec, row_vec],
        out_specs=[pl.BlockSpec((1, rows, d_a), lambda i, j: (i, j, 0)),
                   pl.BlockSpec((1, SUBLANES, d_a), lambda i, j: (i, 0, 0)),
                   pl.BlockSpec((1, 1, d_a), lambda i, j: (i, 0, 0))],
        out_shape=[jax.ShapeDtypeStruct((b, t, d_a), F32),
                   jax.ShapeDtypeStruct((b, SUBLANES, d_a), F32),
                   jax.ShapeDtypeStruct((b, 1, d_a), F32)],
        scratch_shapes=[pltpu.VMEM((1, d_a), F32), pltpu.VMEM((SUBLANES, d_a), F32)],
        compiler_params=_params(("arbitrary", "arbitrary"), 48),
        name="rglru_reset" if reset_first else "rglru_carry",
    )(a2, a2, conv_state8, h0, w_conv, b_conv, w_ra, b_ra, w_rx, b_rx, lam)


def _cumsum_kernel(x_ref, off_ref, o_ref):
    x = x_ref[...]
    n = x.shape[1]
    lane = lax.broadcasted_iota(jnp.int32, x.shape, 1)
    shift = 1
    while shift < n:
        prev = pltpu.roll(x, shift, 1)
        x = jnp.where(lane >= shift, x + prev, x)
        shift *= 2
    o_ref[...] = x + off_ref[...]


def _cumsum_lanes(x, off):
    r, n = x.shape
    return pl.pallas_call(
        _cumsum_kernel,
        grid=(1,),
        in_specs=[pl.BlockSpec((r, n), lambda i: (0, 0)), pl.BlockSpec((r, 1), lambda i: (0, 0))],
        out_specs=pl.BlockSpec((r, n), lambda i: (0, 0)),
        out_shape=jax.ShapeDtypeStruct((r, n), F32),
        compiler_params=_params(("arbitrary",), 32),
        name="cumsum_lanes",
    )(x, off)


def _softmax_pv(s, v_h):
    m = jnp.max(s, axis=-1, keepdims=True)
    p = jnp.exp2(s - m)
    den = jnp.sum(p, axis=-1, keepdims=True)
    return jnp.dot(p.astype(BF16), v_h, preferred_element_type=F32) / den


def _flash_step(scores, values, m_sc, l_sc, acc_sc):
    prev = [(m_sc[h], l_sc[h], acc_sc[h]) for h in range(N_HEADS)]
    new = []
    for (m_prev, l_prev, acc_prev), s, v_h in zip(prev, scores, values):
        m_new = jnp.maximum(m_prev, jnp.max(s, axis=-1, keepdims=True))
        alpha = jnp.exp2(m_prev - m_new)
        p = jnp.exp2(s - m_new)
        new.append((m_new, alpha * l_prev + jnp.sum(p, axis=-1, keepdims=True),
                    alpha * acc_prev + jnp.dot(p.astype(BF16), v_h, preferred_element_type=F32)))
    for h, (m_new, l_new, acc_new) in enumerate(new):
        m_sc[h] = m_new
        l_sc[h] = l_new
        acc_sc[h] = acc_new


def _flash_init(m_sc, l_sc, acc_sc):
    m_sc[...] = jnp.full(m_sc.shape, NEG, F32)
    l_sc[...] = jnp.zeros(l_sc.shape, F32)
    acc_sc[...] = jnp.zeros(acc_sc.shape, F32)


def _flash_finish(o_ref, l_sc, acc_sc):
    for h in range(N_HEADS):
        o_ref[0, :, h * HEAD_DIM:(h + 1) * HEAD_DIM] = acc_sc[h] / l_sc[h]


def _flash_scratch(rows):
    return [pltpu.VMEM((N_HEADS, rows, 1), F32), pltpu.VMEM((N_HEADS, rows, 1), F32),
            pltpu.VMEM((N_HEADS, rows, HEAD_DIM), F32)]


def _rel_row(rel_table):
    near = rel_table[:, 2 * REL_CLIP:]
    far = rel_table[:, :1]
    n_far = REL_ROW // 2 - BAND - REL_CLIP - 1
    row = jnp.concatenate([jnp.broadcast_to(near, (N_HEADS, BAND - REL_CLIP)), rel_table[:, ::-1],
                           jnp.broadcast_to(far, (N_HEADS, n_far)),
                           jnp.broadcast_to(near, (N_HEADS, REL_ROW // 2))], axis=1)
    return row.reshape(N_HEADS, 1, REL_ROW).astype(F32)


def _rel_tile(row_ref, h, rows, cols):
    tile = pltpu.roll(jnp.broadcast_to(row_ref[h], (rows, REL_ROW)), 0, 1, stride=1, stride_axis=0)
    return tile[:, :cols] * LOG2E


def _band_prompt_kernel(*refs, rows, n_prev):
    n_kv = n_prev + 1
    q_ref, k_refs, v_refs = refs[0], refs[1:1 + n_kv], refs[1 + n_kv:1 + 2 * n_kv]
    row_ref, o_ref, bias_sc = refs[1 + 2 * n_kv:]
    cols = n_kv * rows
    i = pl.program_id(1)

    @pl.when(jnp.logical_and(pl.program_id(0) == 0, i == 0))
    def _():
        qc = lax.broadcasted_iota(jnp.int32, (rows, cols), 0) // CHUNK
        kc = lax.broadcasted_iota(jnp.int32, (rows, cols), 1) // CHUNK - (n_prev * rows) // CHUNK
        visible = jnp.logical_and(kc <= qc, kc >= qc - N_PREV_CHUNKS)
        for h in range(N_HEADS):
            bias_sc[h] = jnp.where(visible, _rel_tile(row_ref, h, rows, cols), NEG)

    def attend(clip_start):
        q = q_ref[0]
        k = jnp.concatenate([r[0] for r in k_refs], axis=0)
        v = jnp.concatenate([r[0] for r in v_refs], axis=0)
        for h in range(N_HEADS):
            s = _dot_nt(_head(q, h), _head(k, h)) + bias_sc[h]
            if clip_start:
                col = lax.broadcasted_iota(jnp.int32, (rows, cols), 1)
                s = jnp.where(col >= (n_prev - i) * rows, s, NEG)
            o_ref[0, :, h * HEAD_DIM:(h + 1) * HEAD_DIM] = _softmax_pv(s, _head(v, h))

    @pl.when(i < n_prev)
    def _():
        attend(True)

    @pl.when(i >= n_prev)
    def _():
        attend(False)


def _band_prompt(q, k, v, rel_row):
    b, s, d = q.shape
    rows = BAND_ROWS
    n_prev = BAND // rows
    assert BAND == n_prev * rows and s % rows == 0 and rows + BAND <= REL_ROW // 2
    back = lambda steps: pl.BlockSpec((1, rows, d), lambda i, j: (i, jnp.maximum(j - steps, 0), 0))
    window = [back(steps) for steps in range(n_prev, -1, -1)]
    return pl.pallas_call(
        functools.partial(_band_prompt_kernel, rows=rows, n_prev=n_prev),
        grid=(b, s // rows),
        in_specs=[back(0)] + window + window + [_resident(rel_row.shape)],
        out_specs=back(0),
        out_shape=jax.ShapeDtypeStruct((b, s, d), F32),
        scratch_shapes=[pltpu.VMEM((N_HEADS, rows, (n_prev + 1) * rows), F32)],
        compiler_params=_params(("arbitrary", "arbitrary"), 48),
        name="band_prompt",
    )(q, *[k] * (n_prev + 1), *[v] * (n_prev + 1), rel_row)


def _band_sample_kernel(q_ref, kn_ref, vn_ref, kc_ref, vc_ref, row_ref, o_ref, bias_sc, *, t_new, n_cache):
    @pl.when(pl.program_id(0) == 0)
    def _():
        for h in range(N_HEADS):
            bias_sc[h] = _rel_tile(row_ref, h, t_new, n_cache + t_new)

    q = q_ref[0]
    kn = kn_ref[0].astype(BF16)
    vn = vn_ref[0].astype(BF16)
    for h in range(N_HEADS):
        cached = _head_rows(h, n_cache)
        k = jnp.concatenate([kc_ref[cached, :].astype(BF16), _head(kn, h)], axis=0)
        v = jnp.concatenate([vc_ref[cached, :].astype(BF16), _head(vn, h)], axis=0)
        s = _dot_nt((_head(q, h) * (SCALE * LOG2E)).astype(BF16), k) + bias_sc[h]
        o_ref[0, :, h * HEAD_DIM:(h + 1) * HEAD_DIM] = _softmax_pv(s, v)


def _band_sample(q, k_new, v_new, k_cache, v_cache, layer, rel_row):
    b, t, d = q.shape
    n_c = k_cache.shape[2] // N_HEADS
    assert n_c == BAND
    new = pl.BlockSpec((1, t, d), lambda i: (i, 0, 0))
    old = pl.BlockSpec((None, None, n_c * N_HEADS, HEAD_DIM), lambda i: (layer, i, 0, 0))
    return pl.pallas_call(
        functools.partial(_band_sample_kernel, t_new=t, n_cache=n_c),
        grid=(b,),
        in_specs=[new, new, new, old, old, _resident(rel_row.shape)],
        out_specs=new,
        out_shape=jax.ShapeDtypeStruct((b, t, d), F32),
        scratch_shapes=[pltpu.VMEM((N_HEADS, t, n_c + t), F32)],
        compiler_params=_params(("arbitrary",), 32),
        name="band_sample",
    )(q, k_new, v_new, k_cache, v_cache, rel_row)


def _fox_prompt_kernel(qi_ref, kj_ref, qa_ref, ka_ref, vt_ref, o_ref, m_sc, l_sc, acc_sc, *, rows):
    i = qi_ref[pl.program_id(1)]
    j = kj_ref[pl.program_id(1)]

    @pl.when(j == 0)
    def _():
        _flash_init(m_sc, l_sc, acc_sc)

    def attend(diagonal):
        scores = [_dot_nt(ka_ref[h], qa_ref[h]) for h in range(N_HEADS)]
        if diagonal:
            causal = (lax.broadcasted_iota(jnp.int32, (rows, rows), 0)
                      <= lax.broadcasted_iota(jnp.int32, (rows, rows), 1))
            scores = [jnp.where(causal, s, NEG) for s in scores]
        prev = [(m_sc[h], l_sc[h], acc_sc[h]) for h in range(N_HEADS)]
        new = []
        for h, ((m_prev, l_prev, acc_prev), s) in enumerate(zip(prev, scores)):
            m_new = jnp.maximum(m_prev, jnp.max(s, axis=0, keepdims=True))
            alpha = jnp.exp2(m_prev - m_new)
            p = jnp.exp2(s - m_new)
            new.append((m_new, alpha * l_prev + jnp.sum(p, axis=0, keepdims=True),
                        alpha * acc_prev + jnp.dot(vt_ref[h], p.astype(BF16), preferred_element_type=F32)))
        for h, (m_new, l_new, acc_new) in enumerate(new):
            m_sc[h] = m_new
            l_sc[h] = l_new
            acc_sc[h] = acc_new

    @pl.when(j < i)
    def _():
        attend(False)

    @pl.when(j == i)
    def _():
        attend(True)
        for h in range(N_HEADS):
            o_ref[0, :, h * HEAD_DIM:(h + 1) * HEAD_DIM] = (acc_sc[h] / l_sc[h]).T


def _fox_prompt(qa, ka, vt, batch):
    _, t, da = qa.shape
    rows = FOX_ROWS
    n = t // batch // rows
    pairs = [(i, j) for i in range(n) for j in range(i + 1)]
    qi = jnp.asarray([p[0] for p in pairs], jnp.int32)
    kj = jnp.asarray([p[1] for p in pairs], jnp.int32)
    grid_spec = pltpu.PrefetchScalarGridSpec(
        num_scalar_prefetch=2,
        grid=(batch, len(pairs)),
        in_specs=[pl.BlockSpec((N_HEADS, rows, da), lambda b, p, qi, kj: (0, b * n + qi[p], 0)),
                  pl.BlockSpec((N_HEADS, rows, da), lambda b, p, qi, kj: (0, b * n + kj[p], 0)),
                  pl.BlockSpec((N_HEADS, HEAD_DIM, rows), lambda b, p, qi, kj: (0, 0, b * n + kj[p]))],
        out_specs=pl.BlockSpec((1, rows, D_ATT), lambda b, p, qi, kj: (b, qi[p], 0)),
        scratch_shapes=[pltpu.VMEM((N_HEADS, 1, rows), F32), pltpu.VMEM((N_HEADS, 1, rows), F32),
                        pltpu.VMEM((N_HEADS, HEAD_DIM, rows), F32)])
    return pl.pallas_call(
        functools.partial(_fox_prompt_kernel, rows=rows),
        grid_spec=grid_spec,
        out_shape=jax.ShapeDtypeStruct((batch, t // batch, D_ATT), F32),
        compiler_params=_params(("arbitrary", "arbitrary"), 48),
        name="fox_prompt",
    )(qi, kj, qa, ka, vt)


def _fox_sample_kernel(*refs, t_new, n_parts):
    q_ref, kc_refs, vc_refs = refs[0], refs[1:1 + n_parts], refs[1 + n_parts:1 + 2 * n_parts]
    kn_ref, vn_ref, fq_ref, fkc_ref, fkn_ref, o_ref, m_sc, l_sc, acc_sc = refs[1 + 2 * n_parts:]
    j = pl.program_id(1)

    @pl.when(j == 0)
    def _():
        _flash_init(m_sc, l_sc, acc_sc)

    q = q_ref[0]
    qs = [(_head(q, h) * (SCALE * LOG2E)).astype(BF16) for h in range(N_HEADS)]
    fq = fq_ref[0] * LOG2E

    def bias(fk, h):
        return fq[:, h:h + 1] - fk[h:h + 1, :] * LOG2E

    fkc = fkc_ref[0]
    part_rows = fkc.shape[1] // n_parts

    def cached(part_refs, h):
        rows_h = _head_rows(h, part_rows)
        return jnp.concatenate([r[rows_h, :] for r in part_refs], axis=0).astype(BF16)

    _flash_step([_dot_nt(qs[h], cached(kc_refs, h)) + bias(fkc, h) for h in range(N_HEADS)],
                [cached(vc_refs, h) for h in range(N_HEADS)], m_sc, l_sc, acc_sc)

    @pl.when(j == pl.num_programs(1) - 1)
    def _():
        kn = kn_ref[0].astype(BF16)
        vn = vn_ref[0].astype(BF16)
        fkn = fkn_ref[0]
        causal = (lax.broadcasted_iota(jnp.int32, (t_new, t_new), 1)
                  <= lax.broadcasted_iota(jnp.int32, (t_new, t_new), 0))
        _flash_step([jnp.where(causal, _dot_nt(qs[h], _head(kn, h)) + bias(fkn, h), NEG) for h in range(N_HEADS)],
                    [_head(vn, h) for h in range(N_HEADS)], m_sc, l_sc, acc_sc)
        _flash_finish(o_ref, l_sc, acc_sc)


def _fox_sample(q, k_new, v_new, k_cache, v_cache, layer, fq, fk_cache, fk_new):
    b, t, d = q.shape
    p = k_cache.shape[2] // N_HEADS
    rows = min(FOX_CACHE_ROWS, p)
    new = pl.BlockSpec((1, t, d), lambda i, j: (i, 0, 0))
    n_parts = FOX_CACHE_PARTS
    part = lambda n: pl.BlockSpec((None, None, rows // n_parts * N_HEADS, HEAD_DIM),
                                  lambda i, j: (layer, i, j * n_parts + n, 0))
    old = [part(n) for n in range(n_parts)]
    return pl.pallas_call(
        functools.partial(_fox_sample_kernel, t_new=t, n_parts=n_parts),
        grid=(b, p // rows),
        in_specs=[new, *old, *old, new, new,
                  pl.BlockSpec((1, t, N_HEADS), lambda i, j: (i, 0, 0)),
                  pl.BlockSpec((1, N_HEADS, rows), lambda i, j: (i, 0, j)),
                  pl.BlockSpec((1, N_HEADS, t), lambda i, j: (i, 0, 0))],
        out_specs=new,
        out_shape=jax.ShapeDtypeStruct((b, t, d), F32),
        scratch_shapes=_flash_scratch(t),
        compiler_params=_params(("arbitrary", "arbitrary"), 48),
        name="fox_sample",
    )(q, *[k_cache] * n_parts, *[v_cache] * n_parts, k_new, v_new, fq, fk_cache, fk_new)


def _mix_kernel(x_ref, ya_ref, yb_ref, yc_ref, g_ref, w_ref, lg_ref, lb_ref, o_ref, *, alpha):
    d_a = ya_ref.shape[-1]
    d_b = yb_ref.shape[-1]
    g = g_ref[...]
    y = jnp.concatenate([_rms_norm(ya_ref[...], g[:, :d_a]),
                         _rms_norm(yb_ref[...], g[:, d_a:d_a + d_b]),
                         _rms_norm(yc_ref[...], g[:, d_a + d_b:])], axis=1)
    z = alpha * x_ref[...] + jnp.dot(y.astype(BF16), w_ref[...], preferred_element_type=F32)
    o_ref[...] = _layer_norm(z, lg_ref[...], lb_ref[...])


def _mix(x2d, ya, yb, yc, g_mix, w_out_all, layer, ln_g, ln_b, alpha):
    t, d = x2d.shape
    rows = min(MIX_ROWS, t)
    tok = lambda w: pl.BlockSpec((rows, w), lambda i: (i, 0))
    vec = _resident((1, d))
    return pl.pallas_call(
        functools.partial(_mix_kernel, alpha=alpha),
        grid=(t // rows,),
        in_specs=[tok(d), tok(ya.shape[1]), tok(yb.shape[1]), tok(yc.shape[1]), vec,
                  _layer_block(layer, w_out_all.shape[1:]), vec, vec],
        out_specs=tok(d),
        out_shape=jax.ShapeDtypeStruct((t, d), F32),
        compiler_params=_params(("arbitrary",), 48),
        name="mix",
    )(x2d, ya, yb, yc, g_mix, w_out_all, ln_g, ln_b)


def _ffn_kernel(x_ref, wu_ref, wd_ref, lg_ref, lb_ref, o_ref, xb_sc, *, alpha):
    j = pl.program_id(1)
    last = pl.num_programs(1) - 1
    rows = o_ref.shape[0]
    part = rows // FFN_EDGE_PARTS
    parts = [slice(r * part, (r + 1) * part) for r in range(FFN_EDGE_PARTS)]

    def term(sl):
        total = None
        for c in range(wu_ref.shape[1] // FFN_CHUNK):
            cols = slice(c * FFN_CHUNK, (c + 1) * FFN_CHUNK)
            hid = jnp.maximum(jnp.dot(xb_sc[sl, :], wu_ref[:, cols], preferred_element_type=F32), 0.0)
            part_sum = jnp.dot((hid * hid).astype(BF16), wd_ref[cols, :], preferred_element_type=F32)
            total = part_sum if total is None else total + part_sum
        return total

    @pl.when(j == 0)
    def _():
        for sl in parts:
            xb_sc[sl, :] = x_ref[sl, :].astype(BF16)
            o_ref[sl, :] = term(sl)

    @pl.when(jnp.logical_and(j > 0, j < last))
    def _():
        o_ref[...] += term(slice(None))

    @pl.when(j == last)
    def _():
        for sl in parts:
            o_ref[sl, :] = _layer_norm(alpha * x_ref[sl, :] + (o_ref[sl, :] + term(sl)), lg_ref[...], lb_ref[...])


def _ffn(x2d, w_up_all, w_down_all, layer, ln_g, ln_b, alpha):
    t, d = x2d.shape
    d_ff = w_up_all.shape[2]
    rows, cols = min(FFN_ROWS, t), FFN_COLS
    vec = _resident((1, d))
    return pl.pallas_call(
        functools.partial(_ffn_kernel, alpha=alpha),
        grid=(t // rows, d_ff // cols),
        in_specs=[pl.BlockSpec((rows, d), lambda i, j: (i, 0)),
                  pl.BlockSpec((None, d, cols), lambda i, j: (layer, 0, j)),
                  pl.BlockSpec((None, cols, d), lambda i, j: (layer, j, 0)),
                  vec, vec],
        out_specs=pl.BlockSpec((rows, d), lambda i, j: (i, 0)),
        out_shape=jax.ShapeDtypeStruct((t, d), F32),
        scratch_shapes=[pltpu.VMEM((rows, d), BF16)],
        compiler_params=_params(("arbitrary", "arbitrary"), 58),
        name="ffn",
    )(x2d, w_up_all, w_down_all, ln_g, ln_b)


def _cum_logf(logf_t, off):
    b, h, t = logf_t.shape
    t_pad = -(-t // LANES) * LANES
    x = logf_t.reshape(b * h, t)
    if t_pad != t:
        x = jnp.pad(x, ((0, 0), (0, t_pad - t)))
    return _cumsum_lanes(x, off.reshape(b * h, 1))[:, :t].reshape(b, h, t)


def _row(v):
    return v.reshape(1, -1)


def kernel(x_prompt, x_sample, state_conv, state_lru, cache_band_k, cache_band_v, cache_fox_k, cache_fox_v,
           cache_fox_logf, w_in, w_conv, b_conv, w_rg_a, b_rg_a, w_rg_x, b_rg_x, lru_lambda, rel_bias,
           b_forget, g_mix, w_out, ln1_g, ln1_b, w_up, w_down, ln2_g, ln2_b):
    depth = w_in.shape[0]
    bp, s, d = x_prompt.shape
    bs, ts, _ = x_sample.shape
    d_a = w_conv.shape[-1]
    alpha = float((2 * depth) ** 0.25)
    d_main = 2 * d_a + 6 * D_ATT
    widths = (2 * d_a,) + (D_ATT,) * 6 + (LANES,)
    n_band_p = min(BAND, s)
    p_len = cache_fox_k.shape[2]

    w_in_t = jnp.transpose(w_in, (2, 0, 1))
    w_in_b = w_in_t[:d_main].reshape(d_main, depth * d).astype(BF16)
    w_f_b = jnp.pad(w_in_t[d_main:].reshape(N_HEADS, depth * d),
                    ((0, LANES - N_HEADS), (0, 0))).astype(BF16)
    w_ra_b, w_rx_b = w_rg_a.astype(BF16), w_rg_x.astype(BF16)
    w_out_b, w_up_b, w_down_b = w_out.astype(BF16), w_up.astype(BF16), w_down.astype(BF16)

    xp = x_prompt.reshape(bp * s, d)
    xs = x_sample.reshape(bs * ts, d)
    outs = [[] for _ in range(6)]
    p_kv = [jnp.zeros((depth, bp, rows_ * N_HEADS, HEAD_DIM), F32) for rows_ in (s, s, n_band_p, n_band_p)]
    s_kv = [jnp.zeros((depth, bs, ts * N_HEADS, HEAD_DIM), F32) for _ in range(4)]
    band_kc, band_vc = _interleaved(cache_band_k), _interleaved(cache_band_v)
    fox_kc, fox_vc = _interleaved(cache_fox_k), _interleaved(cache_fox_v)

    for l in range(depth):
        bf_pad = jnp.pad(b_forget[l], (0, LANES - N_HEADS)).reshape(1, LANES)
        lru_w = (w_conv[l], _row(b_conv[l]), w_ra_b[l], _row(b_rg_a[l]), w_rx_b[l], _row(b_rg_x[l]),
                 _row(lru_lambda[l]))
        rel_row = _rel_row(rel_bias[l])

        def finish(x2d, ya, yb, yc):
            x1 = _mix(x2d, ya, yb, yc, _row(g_mix[l]), w_out_b, l, _row(ln1_g[l]), _row(ln1_b[l]), alpha)
            return _ffn(x1, w_up_b, w_down_b, l, _row(ln2_g[l]), _row(ln2_b[l]), alpha)

        a2, bq, bk, bv, lf, qa, ka, vt, *p_kv = _proj_prompt(xp, w_in_b, w_f_b, l, bf_pad, widths, p_kv, s)
        ya, buf, h_last = _rglru(a2.reshape(bp, s, 2 * d_a), jnp.zeros((bp, SUBLANES, d_a), F32),
                                 jnp.zeros((bp, 1, d_a), F32), *lru_w, reset_first=True)
        r3 = lambda z: z.reshape(bp, s, D_ATT)
        yb = _band_prompt(r3(bq), r3(bk), r3(bv), rel_row)
        yc = _fox_prompt(qa, ka, vt, bp)
        xp = finish(xp, ya.reshape(bp * s, d_a), yb.reshape(bp * s, D_ATT), yc.reshape(bp * s, D_ATT))
        for dst, val in zip(outs[:3], (buf[:, SUBLANES - (CONV_W - 1):], h_last.reshape(bp, d_a),
                                       lf[:, :N_HEADS].reshape(bp, s, N_HEADS))):
            dst.append(val)

        a2, bq, bk, bv, lf, cq, ck, cv, *s_kv = _proj_sample(xs, w_in_b, w_f_b, l, bf_pad, widths, s_kv, ts)
        conv8 = jnp.pad(state_conv[l], ((0, 0), (SUBLANES - (CONV_W - 1), 0), (0, 0)))
        ya, buf, h_last = _rglru(a2.reshape(bs, ts, 2 * d_a), conv8, state_lru[l].reshape(bs, 1, d_a),
                                 *lru_w, reset_first=False)
        r3 = lambda z: z.reshape(bs, ts, D_ATT)
        yb = _band_sample(r3(bq), r3(bk), r3(bv), band_kc, band_vc, l, rel_row)
        logf = lf[:, :N_HEADS].reshape(bs, ts, N_HEADS)
        fk_cache = _cum_logf(jnp.swapaxes(cache_fox_logf[l], 1, 2), jnp.zeros((bs, N_HEADS, 1), F32))
        fk_new = _cum_logf(jnp.swapaxes(logf, 1, 2), fk_cache[:, :, p_len - 1:])
        yc = _fox_sample(r3(cq), r3(ck), r3(cv), fox_kc, fox_vc, l,
                         jnp.swapaxes(fk_new, 1, 2), fk_cache, fk_new)
        xs = finish(xs, ya.reshape(bs * ts, d_a), yb.reshape(bs * ts, D_ATT), yc.reshape(bs * ts, D_ATT))
        for dst, val in zip(outs[3:], (buf[:, SUBLANES - (CONV_W - 1):], h_last.reshape(bs, d_a), logf)):
            dst.append(val)

    st = [jnp.stack(o, axis=0) for o in outs]
    heads = lambda z: z.reshape(z.shape[:2] + (z.shape[2] // N_HEADS, N_HEADS, HEAD_DIM))
    fox_k, fox_v, band_k, band_v = [heads(z) for z in p_kv]
    return (xp.reshape(bp, s, d), xs.reshape(bs, ts, d),
            st[0], st[1], band_k, band_v, fox_k, fox_v, st[2],
            st[3], st[4], *[heads(z) for z in s_kv], st[5])
```

```python
import functools

import numpy as np
import jax
import jax.numpy as jnp
from jax import lax
from jax.experimental import pallas as pl
from jax.experimental.pallas import tpu as pltpu

F32 = jnp.float32
BF16 = jnp.bfloat16

HEAD_DIM = 128
CHUNK = 64
N_PREV_CHUNKS = 8
BAND = N_PREV_CHUNKS * CHUNK
REL_CLIP = 256
CONV_W = 4
LRU_C = 8.0
LN_EPS = 1e-5
NEG = -1e30
SCALE = HEAD_DIM ** -0.5
LOG2E = float(np.log2(np.e))
N_HEADS = 4
D_ATT = N_HEADS * HEAD_DIM
LANES = 128
SUBLANES = 8
MIB = 1 << 20
N_SPLIT = 3
REL_ROW = 4 * BAND

PROJ_ROWS = 256
LRU_ROWS = 512
BAND_ROWS = 256
FOX_ROWS = 512
FOX_CACHE_ROWS = 2048
FOX_CACHE_PARTS = 1
MIX_ROWS = 512
FFN_ROWS = 512
FFN_COLS = 2048
FFN_CHUNK = 1024
FFN_EDGE_PARTS = 4


def _params(semantics, vmem_mib):
    return pltpu.CompilerParams(dimension_semantics=semantics, vmem_limit_bytes=vmem_mib * MIB)


def _resident(shape):
    return pl.BlockSpec(shape, lambda *_: (0,) * len(shape), pipeline_mode=pl.Buffered(1))


def _layer_block(layer, shape):
    return pl.BlockSpec((None,) + tuple(shape), lambda *_: (layer,) + (0,) * len(shape),
                        pipeline_mode=pl.Buffered(1))


def _log_sigmoid(z):
    return jnp.minimum(z, 0.0) - jnp.log1p(jnp.exp(-jnp.abs(z)))


def _softplus(z):
    return jnp.maximum(z, 0.0) + jnp.log1p(jnp.exp(-jnp.abs(z)))


def _gelu_tanh(x):
    c = np.sqrt(2.0 / np.pi).astype(np.float32)
    return 0.5 * x * (1.0 + jnp.tanh(c * (x + 0.044715 * (x * x * x))))


def _layer_norm(z, g, b):
    mu = jnp.mean(z, axis=-1, keepdims=True)
    zc = z - mu
    var = jnp.mean(zc * zc, axis=-1, keepdims=True)
    return zc * lax.rsqrt(var + LN_EPS) * g + b


def _rms_norm(y, g):
    return y * lax.rsqrt(jnp.mean(y * y, axis=-1, keepdims=True) + LN_EPS) * g


def _dot_nt(a, b):
    return lax.dot_general(a, b, (((1,), (1,)), ((), ())), preferred_element_type=F32)


def _head(x, h):
    return x[:, h * HEAD_DIM:(h + 1) * HEAD_DIM]


def _head_rows(h, rows):
    return pl.ds(h, rows, stride=N_HEADS)


def _interleaved(a):
    return a.reshape(a.shape[:-3] + (a.shape[-3] * N_HEADS, HEAD_DIM))


def _cumsum_rows(x):
    row = lax.broadcasted_iota(jnp.int32, x.shape, 0)
    shift = 1
    while shift < x.shape[0]:
        x = jnp.where(row >= shift, x + pltpu.roll(x, shift, 0), x)
        shift *= 2
    return x


def _split_bf16(x):
    pieces = []
    for _ in range(N_SPLIT):
        p = x.astype(BF16).astype(F32)
        pieces.append(p)
        x = x - p
    return pieces


def _proj_kernel(*refs, widths, fox_aug, stream_rows, n_aliased):
    x_ref, w_ref, wf_ref, bf_ref = refs[:4]
    if fox_aug:
        (a2_ref, bq_ref, bk_ref, bv_ref, lf_ref,
         qa_ref, ka_ref, vt_ref, k5_ref, v5_ref, bk5_ref, bv5_ref, f_sc) = refs[4 + n_aliased:]
    else:
        (a2_ref, bq_ref, bk_ref, bv_ref, lf_ref, cq_ref, ck_ref, cv_ref,
         bk5_ref, bv5_ref, ck5_ref, cv5_ref) = refs[4 + n_aliased:]
    xb = x_ref[...].astype(BF16)
    rows = xb.shape[0]
    offs = np.concatenate([[0], np.cumsum(widths)])

    def group(g):
        return _dot_nt(xb, w_ref[int(offs[g]):int(offs[g + 1]), :])

    logf = _log_sigmoid(_dot_nt(xb, wf_ref[...]) + bf_ref[...])
    lf_ref[...] = logf
    cq, ck, cv = group(4), group(5), group(6)
    bq, bk, bv = group(1), group(2), group(3)
    a2_ref[...] = group(0)
    if not fox_aug:
        bq_ref[...] = bq
        bk_ref[...] = bk
        bv_ref[...] = bv
        cq_ref[...] = cq
        ck_ref[...] = ck
        cv_ref[...] = cv
        for dst_ref, val in ((bk5_ref, bk), (bv5_ref, bv), (ck5_ref, ck), (cv5_ref, cv)):
            for b in range(rows // stream_rows):
                for h in range(N_HEADS):
                    dst_ref[b, _head_rows(h, stream_rows), :] = _head(val[b * stream_rows:(b + 1) * stream_rows], h)
        return

    bq_ref[...] = (bq * (SCALE * LOG2E)).astype(BF16)
    bk_ref[...] = bk.astype(BF16)
    bv_ref[...] = bv.astype(BF16)
    blocks = stream_rows // rows
    block = pl.program_id(0) % blocks

    @pl.when(block >= blocks - min(BAND, stream_rows) // rows)
    def _():
        for h in range(N_HEADS):
            bk5_ref[_head_rows(h, rows), :] = _head(bk, h)
            bv5_ref[_head_rows(h, rows), :] = _head(bv, h)

    @pl.when(block == 0)
    def _():
        f_sc[...] = jnp.zeros(f_sc.shape, F32)

    cum = _cumsum_rows(logf) + f_sc[...]
    f_sc[...] = cum[rows - 1:rows]
    cum2 = cum * LOG2E
    lane = lax.broadcasted_iota(jnp.int32, (rows, HEAD_DIM), 1)
    for h in range(N_HEADS):
        pieces = _split_bf16(jnp.broadcast_to(cum2[:, h:h + 1], (rows, HEAD_DIM)))
        q_extra = jnp.where(lane < 2 * N_SPLIT, 1.0, 0.0)
        k_extra = jnp.where(lane < N_SPLIT, 1.0, 0.0)
        for n, piece in enumerate(pieces):
            q_extra = jnp.where(lane == n, piece, q_extra)
            k_extra = jnp.where(lane == N_SPLIT + n, -piece, k_extra)
        qa_ref[h, :, :HEAD_DIM] = (_head(cq, h) * (SCALE * LOG2E)).astype(BF16)
        qa_ref[h, :, HEAD_DIM:] = q_extra.astype(BF16)
        ka_ref[h, :, :HEAD_DIM] = _head(ck, h).astype(BF16)
        ka_ref[h, :, HEAD_DIM:] = k_extra.astype(BF16)
        vt_ref[h] = _head(cv, h).T.astype(BF16)
        k5_ref[_head_rows(h, rows), :] = _head(ck, h)
        v5_ref[_head_rows(h, rows), :] = _head(cv, h)


def _proj_sample(x2d, w_all, wf_all, layer, bf_pad, widths, kv_all, seq):
    t, d = x2d.shape
    rows = min(PROJ_ROWS, t)
    order = (0, 1, 2, 3, 7, 4, 5, 6)
    final = pl.BlockSpec((None, rows // seq, seq * N_HEADS, HEAD_DIM), lambda i: (layer, i, 0, 0))
    n_in = 4
    return pl.pallas_call(
        functools.partial(_proj_kernel, widths=widths, fox_aug=False, stream_rows=seq, n_aliased=len(kv_all)),
        grid=(t // rows,),
        in_specs=[pl.BlockSpec((rows, d), lambda i: (i, 0)), pl.BlockSpec((sum(widths[:-1]), d), lambda i: (0, layer), pipeline_mode=pl.Buffered(1)),
                  pl.BlockSpec((widths[-1], d), lambda i: (0, layer), pipeline_mode=pl.Buffered(1)), _resident((1, widths[-1]))]
                 + [pl.BlockSpec(memory_space=pl.ANY)] * len(kv_all),
        out_specs=[pl.BlockSpec((rows, widths[g]), lambda i: (i, 0)) for g in order] + [final] * len(kv_all),
        out_shape=[jax.ShapeDtypeStruct((t, widths[g]), F32) for g in order]
                  + [jax.ShapeDtypeStruct(a.shape, a.dtype) for a in kv_all],
        input_output_aliases={n_in + n: len(order) + n for n in range(len(kv_all))},
        compiler_params=_params(("arbitrary",), 48),
        name="proj_sample",
    )(x2d, w_all, wf_all, bf_pad, *kv_all)


def _proj_prompt(x2d, w_all, wf_all, layer, bf_pad, widths, kv_all, seq):
    t, d = x2d.shape
    rows = PROJ_ROWS
    bpb = seq // rows
    first_new = bpb - min(BAND, seq) // rows
    tok = lambda w, dt: (pl.BlockSpec((rows, w), lambda i: (i, 0)), jax.ShapeDtypeStruct((t, w), dt))
    aug = (pl.BlockSpec((N_HEADS, rows, 2 * HEAD_DIM), lambda i: (0, i, 0)),
           jax.ShapeDtypeStruct((N_HEADS, t, 2 * HEAD_DIM), BF16))
    every = pl.BlockSpec((None, None, rows * N_HEADS, HEAD_DIM), lambda i: (layer, i // bpb, i % bpb, 0))
    newest = pl.BlockSpec((None, None, rows * N_HEADS, HEAD_DIM),
                          lambda i: (layer, i // bpb, jnp.maximum(i % bpb - first_new, 0), 0))
    transposed = (pl.BlockSpec((N_HEADS, HEAD_DIM, rows), lambda i: (0, 0, i)),
                  jax.ShapeDtypeStruct((N_HEADS, HEAD_DIM, t), BF16))
    outs = [tok(widths[0], F32), tok(D_ATT, BF16), tok(D_ATT, BF16), tok(D_ATT, BF16), tok(widths[-1], F32),
            aug, aug, transposed] + [(spec, jax.ShapeDtypeStruct(a.shape, a.dtype))
                                     for spec, a in zip((every, every, newest, newest), kv_all)]
    n_in = 4
    return pl.pallas_call(
        functools.partial(_proj_kernel, widths=widths, fox_aug=True, stream_rows=seq, n_aliased=len(kv_all)),
        grid=(t // rows,),
        in_specs=[pl.BlockSpec((rows, d), lambda i: (i, 0)), pl.BlockSpec((sum(widths[:-1]), d), lambda i: (0, layer), pipeline_mode=pl.Buffered(1)),
                  pl.BlockSpec((widths[-1], d), lambda i: (0, layer), pipeline_mode=pl.Buffered(1)), _resident((1, widths[-1]))]
                 + [pl.BlockSpec(memory_space=pl.ANY)] * len(kv_all),
        out_specs=[o[0] for o in outs],
        out_shape=[o[1] for o in outs],
        scratch_shapes=[pltpu.VMEM((1, widths[-1]), F32)],
        input_output_aliases={n_in + n: len(outs) - len(kv_all) + n for n in range(len(kv_all))},
        compiler_params=_params(("arbitrary",), 52),
        name="proj_prompt",
    )(x2d, w_all, wf_all, bf_pad, *kv_all)


def _rglru_block(xa, ag, lru_refs, h_sc, tail_sc, first_block, reset_first):
    wc_ref, bc_ref, wra_ref, bra_ref, wrx_ref, brx_ref, lam_ref = lru_refs
    rows, d_a = xa.shape
    ext = jnp.concatenate([tail_sc[...], xa], axis=0)
    xc = bc_ref[...] + wc_ref[CONV_W - 1:CONV_W, :] * xa
    for j in range(1, CONV_W):
        shifted = pltpu.roll(ext, j, 0)[SUBLANES:SUBLANES + rows]
        xc = xc + wc_ref[CONV_W - 1 - j:CONV_W - j, :] * shifted
    tail_sc[...] = xa[rows - SUBLANES:rows]

    xcb = xc.astype(BF16)

    def gate(w_ref, b_ref):
        parts = [jnp.dot(xcb[:, n * HEAD_DIM:(n + 1) * HEAD_DIM], w_ref[n], preferred_element_type=F32)
                 for n in range(d_a // HEAD_DIM)]
        return jax.nn.sigmoid(jnp.concatenate(parts, axis=1) + b_ref[...])

    r = gate(wra_ref, bra_ref)
    i_gate = gate(wrx_ref, brx_ref)
    log_a = (-LRU_C) * r * _softplus(-lam_ref[...])
    a = jnp.exp(log_a)
    mult = jnp.exp(0.5 * jnp.log(1.0 - a * a))
    row = lax.broadcasted_iota(jnp.int32, (rows, d_a), 0)
    if reset_first:
        mult = jnp.where(jnp.logical_and(row == 0, first_block), 1.0, mult)
    u = mult * (i_gate * xc)

    n_groups = rows // SUBLANES
    a = a.reshape(n_groups, SUBLANES, d_a)
    u = u.reshape(n_groups, SUBLANES, d_a)
    in_group = lax.broadcasted_iota(jnp.int32, a.shape, 1)
    shift = 1
    while shift < SUBLANES:
        a_prev = pltpu.roll(a, shift, 1)
        u_prev = pltpu.roll(u, shift, 1)
        live = in_group >= shift
        u = jnp.where(live, a * u_prev + u, u)
        a = jnp.where(live, a * a_prev, a)
        shift *= 2
    h_last = h_sc[...]
    groups = []
    for g in range(n_groups):
        h_g = u[g] + a[g] * h_last
        h_last = h_g[SUBLANES - 1:SUBLANES]
        groups.append(h_g)
    h = jnp.concatenate(groups, axis=0)
    h_sc[...] = h_last
    return h * _gelu_tanh(ag), h_last


def _rglru_kernel(ax_ref, ag_ref, cs_ref, h0_ref, *rest, rows, reset_first):
    lru_refs, (y_ref, buf_ref, hl_ref, h_sc, tail_sc) = rest[:7], rest[7:]
    t = pl.program_id(1)

    @pl.when(t == 0)
    def _():
        h_sc[...] = h0_ref[0]
        tail_sc[...] = cs_ref[0]

    xa = ax_ref[0]
    y_ref[0], h_last = _rglru_block(xa, ag_ref[0], lru_refs, h_sc, tail_sc, t == 0, reset_first)

    @pl.when(t == pl.num_programs(1) - 1)
    def _():
        buf_ref[0] = xa[rows - SUBLANES:rows]
        hl_ref[0] = h_last


def _rglru(a2, conv_state8, h0, w_conv, b_conv, w_ra, b_ra, w_rx, b_rx, lam, *, reset_first):
    b, t, d2 = a2.shape
    d_a = d2 // 2
    rows = min(LRU_ROWS, t)
    n_blocks = d_a // HEAD_DIM
    row_vec = _resident((1, d_a))
    gate_w = _resident((n_blocks, HEAD_DIM, HEAD_DIM))
    kern = functools.partial(_rglru_kernel, rows=rows, reset_first=reset_first)
    return pl.pallas_call(
        kern,
        grid=(b, t // rows),
        in_specs=[pl.BlockSpec((1, rows, d_a), lambda i, j: (i, j, 0)),
                  pl.BlockSpec((1, rows, d_a), lambda i, j: (i, j, 1)),
                  pl.BlockSpec((1, SUBLANES, d_a), lambda i, j: (i, 0, 0)),
                  pl.BlockSpec((1, 1, d_a), lambda i, j: (i, 0, 0)),
                  _resident((CONV_W, d_a)), row_vec, gate_w, row_vec, gate_w, row_vec, row_vec],
        out_specs=[pl.BlockSpec((1, rows, d_a), lambda i, j: (i, j, 0)),
                   pl.BlockSpec((1, SUBLANES, d_a), lambda i, j: (i, 0, 0)),
                   pl.BlockSpec((1, 1, d_a), lambda i, j: (i, 0, 0))],
        out_shape=[jax.ShapeDtypeStruct((b, t, d_a), F32),
                   jax.ShapeDtypeStruct((b, SUBLANES, d_a), F32),
                   jax.ShapeDtypeStruct((b, 1, d_a), F32)],
        scratch_shapes=[pltpu.VMEM((1, d_a), F32), pltpu.VMEM((SUBLANES, d_a), F32)],
        compiler_params=_params(("arbitrary", "arbitrary"), 48),
        name="rglru_reset" if reset_first else "rglru_carry",
    )(a2, a2, conv_state8, h0, w_conv, b_conv, w_ra, b_ra, w_rx, b_rx, lam)


def _cumsum_kernel(x_ref, off_ref, o_ref):
    x = x_ref[...]
    n = x.shape[1]
    lane = lax.broadcasted_iota(jnp.int32, x.shape, 1)
    shift = 1
    while shift < n:
        prev = pltpu.roll(x, shift, 1)
        x = jnp.where(lane >= shift, x + prev, x)
        shift *= 2
    o_ref[...] = x + off_ref[...]


def _cumsum_lanes(x, off):
    r, n = x.shape
    return pl.pallas_call(
        _cumsum_kernel,
        grid=(1,),
        in_specs=[pl.BlockSpec((r, n), lambda i: (0, 0)), pl.BlockSpec((r, 1), lambda i: (0, 0))],
        out_specs=pl.BlockSpec((r, n), lambda i: (0, 0)),
        out_shape=jax.ShapeDtypeStruct((r, n), F32),
        compiler_params=_params(("arbitrary",), 32),
        name="cumsum_lanes",
    )(x, off)


def _softmax_pv(s, v_h):
    m = jnp.max(s, axis=-1, keepdims=True)
    p = jnp.exp2(s - m)
    den = jnp.sum(p, axis=-1, keepdims=True)
    return jnp.dot(p.astype(BF16), v_h, preferred_element_type=F32) / den


def _flash_step(scores, values, m_sc, l_sc, acc_sc):
    prev = [(m_sc[h], l_sc[h], acc_sc[h]) for h in range(N_HEADS)]
    new = []
    for (m_prev, l_prev, acc_prev), s, v_h in zip(prev, scores, values):
        m_new = jnp.maximum(m_prev, jnp.max(s, axis=-1, keepdims=True))
        alpha = jnp.exp2(m_prev - m_new)
        p = jnp.exp2(s - m_new)
        new.append((m_new, alpha * l_prev + jnp.sum(p, axis=-1, keepdims=True),
                    alpha * acc_prev + jnp.dot(p.astype(BF16), v_h, preferred_element_type=F32)))
    for h, (m_new, l_new, acc_new) in enumerate(new):
        m_sc[h] = m_new
        l_sc[h] = l_new
        acc_sc[h] = acc_new


def _flash_init(m_sc, l_sc, acc_sc):
    m_sc[...] = jnp.full(m_sc.shape, NEG, F32)
    l_sc[...] = jnp.zeros(l_sc.shape, F32)
    acc_sc[...] = jnp.zeros(acc_sc.shape, F32)


def _flash_finish(o_ref, l_sc, acc_sc):
    for h in range(N_HEADS):
        o_ref[0, :, h * HEAD_DIM:(h + 1) * HEAD_DIM] = acc_sc[h] / l_sc[h]


def _flash_scratch(rows):
    return [pltpu.VMEM((N_HEADS, rows, 1), F32), pltpu.VMEM((N_HEADS, rows, 1), F32),
            pltpu.VMEM((N_HEADS, rows, HEAD_DIM), F32)]


def _rel_row(rel_table):
    near = rel_table[:, 2 * REL_CLIP:]
    far = rel_table[:, :1]
    n_far = REL_ROW // 2 - BAND - REL_CLIP - 1
    row = jnp.concatenate([jnp.broadcast_to(near, (N_HEADS, BAND - REL_CLIP)), rel_table[:, ::-1],
                           jnp.broadcast_to(far, (N_HEADS, n_far)),
                           jnp.broadcast_to(near, (N_HEADS, REL_ROW // 2))], axis=1)
    return row.reshape(N_HEADS, 1, REL_ROW).astype(F32)


def _rel_tile(row_ref, h, rows, cols):
    tile = pltpu.roll(jnp.broadcast_to(row_ref[h], (rows, REL_ROW)), 0, 1, stride=1, stride_axis=0)
    return tile[:, :cols] * LOG2E


def _band_prompt_kernel(*refs, rows, n_prev):
    n_kv = n_prev + 1
    q_ref, k_refs, v_refs = refs[0], refs[1:1 + n_kv], refs[1 + n_kv:1 + 2 * n_kv]
    row_ref, o_ref, bias_sc = refs[1 + 2 * n_kv:]
    cols = n_kv * rows
    i = pl.program_id(1)

    @pl.when(jnp.logical_and(pl.program_id(0) == 0, i == 0))
    def _():
        qc = lax.broadcasted_iota(jnp.int32, (rows, cols), 0) // CHUNK
        kc = lax.broadcasted_iota(jnp.int32, (rows, cols), 1) // CHUNK - (n_prev * rows) // CHUNK
        visible = jnp.logical_and(kc <= qc, kc >= qc - N_PREV_CHUNKS)
        for h in range(N_HEADS):
            bias_sc[h] = jnp.where(visible, _rel_tile(row_ref, h, rows, cols), NEG)

    def attend(clip_start):
        q = q_ref[0]
        k = jnp.concatenate([r[0] for r in k_refs], axis=0)
        v = jnp.concatenate([r[0] for r in v_refs], axis=0)
        for h in range(N_HEADS):
            s = _dot_nt(_head(q, h), _head(k, h)) + bias_sc[h]
            if clip_start:
                col = lax.broadcasted_iota(jnp.int32, (rows, cols), 1)
                s = jnp.where(col >= (n_prev - i) * rows, s, NEG)
            o_ref[0, :, h * HEAD_DIM:(h + 1) * HEAD_DIM] = _softmax_pv(s, _head(v, h))

    @pl.when(i < n_prev)
    def _():
        attend(True)

    @pl.when(i >= n_prev)
    def _():
        attend(False)


def _band_prompt(q, k, v, rel_row):
    b, s, d = q.shape
    rows = BAND_ROWS
    n_prev = BAND // rows
    assert BAND == n_prev * rows and s % rows == 0 and rows + BAND <= REL_ROW // 2
    back = lambda steps: pl.BlockSpec((1, rows, d), lambda i, j: (i, jnp.maximum(j - steps, 0), 0))
    window = [back(steps) for steps in range(n_prev, -1, -1)]
    return pl.pallas_call(
        functools.partial(_band_prompt_kernel, rows=rows, n_prev=n_prev),
        grid=(b, s // rows),
        in_specs=[back(0)] + window + window + [_resident(rel_row.shape)],
        out_specs=back(0),
        out_shape=jax.ShapeDtypeStruct((b, s, d), F32),
        scratch_shapes=[pltpu.VMEM((N_HEADS, rows, (n_prev + 1) * rows), F32)],
        compiler_params=_params(("arbitrary", "arbitrary"), 48),
        name="band_prompt",
    )(q, *[k] * (n_prev + 1), *[v] * (n_prev + 1), rel_row)


def _band_sample_kernel(q_ref, kn_ref, vn_ref, kc_ref, vc_ref, row_ref, o_ref, bias_sc, *, t_new, n_cache):
    @pl.when(pl.program_id(0) == 0)
    def _():
        for h in range(N_HEADS):
            bias_sc[h] = _rel_tile(row_ref, h, t_new, n_cache + t_new)

    q = q_ref[0]
    kn = kn_ref[0].astype(BF16)
    vn = vn_ref[0].astype(BF16)
    for h in range(N_HEADS):
        cached = _head_rows(h, n_cache)
        k = jnp.concatenate([kc_ref[cached, :].astype(BF16), _head(kn, h)], axis=0)
        v = jnp.concatenate([vc_ref[cached, :].astype(BF16), _head(vn, h)], axis=0)
        s = _dot_nt((_head(q, h) * (SCALE * LOG2E)).astype(BF16), k) + bias_sc[h]
        o_ref[0, :, h * HEAD_DIM:(h + 1) * HEAD_DIM] = _softmax_pv(s, v)


def _band_sample(q, k_new, v_new, k_cache, v_cache, layer, rel_row):
    b, t, d = q.shape
    n_c = k_cache.shape[2] // N_HEADS
    assert n_c == BAND
    new = pl.BlockSpec((1, t, d), lambda i: (i, 0, 0))
    old = pl.BlockSpec((None, None, n_c * N_HEADS, HEAD_DIM), lambda i: (layer, i, 0, 0))
    return pl.pallas_call(
        functools.partial(_band_sample_kernel, t_new=t, n_cache=n_c),
        grid=(b,),
        in_specs=[new, new, new, old, old, _resident(rel_row.shape)],
        out_specs=new,
        out_shape=jax.ShapeDtypeStruct((b, t, d), F32),
        scratch_shapes=[pltpu.VMEM((N_HEADS, t, n_c + t), F32)],
        compiler_params=_params(("arbitrary",), 32),
        name="band_sample",
    )(q, k_new, v_new, k_cache, v_cache, rel_row)


def _fox_prompt_kernel(qi_ref, kj_ref, qa_ref, ka_ref, vt_ref, o_ref, m_sc, l_sc, acc_sc, *, rows):
    i = qi_ref[pl.program_id(1)]
    j = kj_ref[pl.program_id(1)]

    @pl.when(j == 0)
    def _():
        _flash_init(m_sc, l_sc, acc_sc)

    def attend(diagonal):
        scores = [_dot_nt(ka_ref[h], qa_ref[h]) for h in range(N_HEADS)]
        if diagonal:
            causal = (lax.broadcasted_iota(jnp.int32, (rows, rows), 0)
                      <= lax.broadcasted_iota(jnp.int32, (rows, rows), 1))
            scores = [jnp.where(causal, s, NEG) for s in scores]
        prev = [(m_sc[h], l_sc[h], acc_sc[h]) for h in range(N_HEADS)]
        new = []
        for h, ((m_prev, l_prev, acc_prev), s) in enumerate(zip(prev, scores)):
            m_new = jnp.maximum(m_prev, jnp.max(s, axis=0, keepdims=True))
            alpha = jnp.exp2(m_prev - m_new)
            p = jnp.exp2(s - m_new)
            new.append((m_new, alpha * l_prev + jnp.sum(p, axis=0, keepdims=True),
                        alpha * acc_prev + jnp.dot(vt_ref[h], p.astype(BF16), preferred_element_type=F32)))
        for h, (m_new, l_new, acc_new) in enumerate(new):
            m_sc[h] = m_new
            l_sc[h] = l_new
            acc_sc[h] = acc_new

    @pl.when(j < i)
    def _():
        attend(False)

    @pl.when(j == i)
    def _():
        attend(True)
        for h in range(N_HEADS):
            o_ref[0, :, h * HEAD_DIM:(h + 1) * HEAD_DIM] = (acc_sc[h] / l_sc[h]).T


def _fox_prompt(qa, ka, vt, batch):
    _, t, da = qa.shape
    rows = FOX_ROWS
    n = t // batch // rows
    pairs = [(i, j) for i in range(n) for j in range(i + 1)]
    qi = jnp.asarray([p[0] for p in pairs], jnp.int32)
    kj = jnp.asarray([p[1] for p in pairs], jnp.int32)
    grid_spec = pltpu.PrefetchScalarGridSpec(
        num_scalar_prefetch=2,
        grid=(batch, len(pairs)),
        in_specs=[pl.BlockSpec((N_HEADS, rows, da), lambda b, p, qi, kj: (0, b * n + qi[p], 0)),
                  pl.BlockSpec((N_HEADS, rows, da), lambda b, p, qi, kj: (0, b * n + kj[p], 0)),
                  pl.BlockSpec((N_HEADS, HEAD_DIM, rows), lambda b, p, qi, kj: (0, 0, b * n + kj[p]))],
        out_specs=pl.BlockSpec((1, rows, D_ATT), lambda b, p, qi, kj: (b, qi[p], 0)),
        scratch_shapes=[pltpu.VMEM((N_HEADS, 1, rows), F32), pltpu.VMEM((N_HEADS, 1, rows), F32),
                        pltpu.VMEM((N_HEADS, HEAD_DIM, rows), F32)])
    return pl.pallas_call(
        functools.partial(_fox_prompt_kernel, rows=rows),
        grid_spec=grid_spec,
        out_shape=jax.ShapeDtypeStruct((batch, t // batch, D_ATT), F32),
        compiler_params=_params(("arbitrary", "arbitrary"), 48),
        name="fox_prompt",
    )(qi, kj, qa, ka, vt)


def _fox_sample_kernel(*refs, t_new, n_parts):
    q_ref, kc_refs, vc_refs = refs[0], refs[1:1 + n_parts], refs[1 + n_parts:1 + 2 * n_parts]
    kn_ref, vn_ref, fq_ref, fkc_ref, fkn_ref, o_ref, m_sc, l_sc, acc_sc = refs[1 + 2 * n_parts:]
    j = pl.program_id(1)

    @pl.when(j == 0)
    def _():
        _flash_init(m_sc, l_sc, acc_sc)

    q = q_ref[0]
    qs = [(_head(q, h) * (SCALE * LOG2E)).astype(BF16) for h in range(N_HEADS)]
    fq = fq_ref[0] * LOG2E

    def bias(fk, h):
        return fq[:, h:h + 1] - fk[h:h + 1, :] * LOG2E

    fkc = fkc_ref[0]
    part_rows = fkc.shape[1] // n_parts

    def cached(part_refs, h):
        rows_h = _head_rows(h, part_rows)
        return jnp.concatenate([r[rows_h, :] for r in part_refs], axis=0).astype(BF16)

    _flash_step([_dot_nt(qs[h], cached(kc_refs, h)) + bias(fkc, h) for h in range(N_HEADS)],
                [cached(vc_refs, h) for h in range(N_HEADS)], m_sc, l_sc, acc_sc)

    @pl.when(j == pl.num_programs(1) - 1)
    def _():
        kn = kn_ref[0].astype(BF16)
        vn = vn_ref[0].astype(BF16)
        fkn = fkn_ref[0]
        causal = (lax.broadcasted_iota(jnp.int32, (t_new, t_new), 1)
                  <= lax.broadcasted_iota(jnp.int32, (t_new, t_new), 0))
        _flash_step([jnp.where(causal, _dot_nt(qs[h], _head(kn, h)) + bias(fkn, h), NEG) for h in range(N_HEADS)],
                    [_head(vn, h) for h in range(N_HEADS)], m_sc, l_sc, acc_sc)
        _flash_finish(o_ref, l_sc, acc_sc)


def _fox_sample(q, k_new, v_new, k_cache, v_cache, layer, fq, fk_cache, fk_new):
    b, t, d = q.shape
    p = k_cache.shape[2] // N_HEADS
    rows = min(FOX_CACHE_ROWS, p)
    new = pl.BlockSpec((1, t, d), lambda i, j: (i, 0, 0))
    n_parts = FOX_CACHE_PARTS
    part = lambda n: pl.BlockSpec((None, None, rows // n_parts * N_HEADS, HEAD_DIM),
                                  lambda i, j: (layer, i, j * n_parts + n, 0))
    old = [part(n) for n in range(n_parts)]
    return pl.pallas_call(
        functools.partial(_fox_sample_kernel, t_new=t, n_parts=n_parts),
        grid=(b, p // rows),
        in_specs=[new, *old, *old, new, new,
                  pl.BlockSpec((1, t, N_HEADS), lambda i, j: (i, 0, 0)),
                  pl.BlockSpec((1, N_HEADS, rows), lambda i, j: (i, 0, j)),
                  pl.BlockSpec((1, N_HEADS, t), lambda i, j: (i, 0, 0))],
        out_specs=new,
        out_shape=jax.ShapeDtypeStruct((b, t, d), F32),
        scratch_shapes=_flash_scratch(t),
        compiler_params=_params(("arbitrary", "arbitrary"), 48),
        name="fox_sample",
    )(q, *[k_cache] * n_parts, *[v_cache] * n_parts, k_new, v_new, fq, fk_cache, fk_new)


def _mix_kernel(x_ref, ya_ref, yb_ref, yc_ref, g_ref, w_ref, lg_ref, lb_ref, o_ref, *, alpha):
    d_a = ya_ref.shape[-1]
    d_b = yb_ref.shape[-1]
    g = g_ref[...]
    y = jnp.concatenate([_rms_norm(ya_ref[...], g[:, :d_a]),
                         _rms_norm(yb_ref[...], g[:, d_a:d_a + d_b]),
                         _rms_norm(yc_ref[...], g[:, d_a + d_b:])], axis=1)
    z = alpha * x_ref[...] + jnp.dot(y.astype(BF16), w_ref[...], preferred_element_type=F32)
    o_ref[...] = _layer_norm(z, lg_ref[...], lb_ref[...])


def _mix(x2d, ya, yb, yc, g_mix, w_out_all, layer, ln_g, ln_b, alpha):
    t, d = x2d.shape
    rows = min(MIX_ROWS, t)
    tok = lambda w: pl.BlockSpec((rows, w), lambda i: (i, 0))
    vec = _resident((1, d))
    return pl.pallas_call(
        functools.partial(_mix_kernel, alpha=alpha),
        grid=(t // rows,),
        in_specs=[tok(d), tok(ya.shape[1]), tok(yb.shape[1]), tok(yc.shape[1]), vec,
                  _layer_block(layer, w_out_all.shape[1:]), vec, vec],
        out_specs=tok(d),
        out_shape=jax.ShapeDtypeStruct((t, d), F32),
        compiler_params=_params(("arbitrary",), 48),
        name="mix",
    )(x2d, ya, yb, yc, g_mix, w_out_all, ln_g, ln_b)


def _ffn_kernel(x_ref, wu_ref, wd_ref, lg_ref, lb_ref, o_ref, xb_sc, *, alpha):
    j = pl.program_id(1)
    last = pl.num_programs(1) - 1
    rows = o_ref.shape[0]
    part = rows // FFN_EDGE_PARTS
    parts = [slice(r * part, (r + 1) * part) for r in range(FFN_EDGE_PARTS)]

    def term(sl):
        total = None
        for c in range(wu_ref.shape[1] // FFN_CHUNK):
            cols = slice(c * FFN_CHUNK, (c + 1) * FFN_CHUNK)
            hid = jnp.maximum(jnp.dot(xb_sc[sl, :], wu_ref[:, cols], preferred_element_type=F32), 0.0)
            part_sum = jnp.dot((hid * hid).astype(BF16), wd_ref[cols, :], preferred_element_type=F32)
            total = part_sum if total is None else total + part_sum
        return total

    @pl.when(j == 0)
    def _():
        for sl in parts:
            xb_sc[sl, :] = x_ref[sl, :].astype(BF16)
            o_ref[sl, :] = term(sl)

    @pl.when(jnp.logical_and(j > 0, j < last))
    def _():
        o_ref[...] += term(slice(None))

    @pl.when(j == last)
    def _():
        for sl in parts:
            o_ref[sl, :] = _layer_norm(alpha * x_ref[sl, :] + (o_ref[sl, :] + term(sl)), lg_ref[...], lb_ref[...])


def _ffn(x2d, w_up_all, w_down_all, layer, ln_g, ln_b, alpha):
    t, d = x2d.shape
    d_ff = w_up_all.shape[2]
    rows, cols = min(FFN_ROWS, t), FFN_COLS
    vec = _resident((1, d))
    return pl.pallas_call(
        functools.partial(_ffn_kernel, alpha=alpha),
        grid=(t // rows, d_ff // cols),
        in_specs=[pl.BlockSpec((rows, d), lambda i, j: (i, 0)),
                  pl.BlockSpec((None, d, cols), lambda i, j: (layer, 0, j)),
                  pl.BlockSpec((None, cols, d), lambda i, j: (layer, j, 0)),
                  vec, vec],
        out_specs=pl.BlockSpec((rows, d), lambda i, j: (i, 0)),
        out_shape=jax.ShapeDtypeStruct((t, d), F32),
        scratch_shapes=[pltpu.VMEM((rows, d), BF16)],
        compiler_params=_params(("arbitrary", "arbitrary"), 58),
        name="ffn",
    )(x2d, w_up_all, w_down_all, ln_g, ln_b)


def _cum_logf(logf_t, off):
    b, h, t = logf_t.shape
    t_pad = -(-t // LANES) * LANES
    x = logf_t.reshape(b * h, t)
    if t_pad != t:
        x = jnp.pad(x, ((0, 0), (0, t_pad - t)))
    return _cumsum_lanes(x, off.reshape(b * h, 1))[:, :t].reshape(b, h, t)


def _row(v):
    return v.reshape(1, -1)


def kernel(x_prompt, x_sample, state_conv, state_lru, cache_band_k, cache_band_v, cache_fox_k, cache_fox_v,
           cache_fox_logf, w_in, w_conv, b_conv, w_rg_a, b_rg_a, w_rg_x, b_rg_x, lru_lambda, rel_bias,
           b_forget, g_mix, w_out, ln1_g, ln1_b, w_up, w_down, ln2_g, ln2_b):
    depth = w_in.shape[0]
    bp, s, d = x_prompt.shape
    bs, ts, _ = x_sample.shape
    d_a = w_conv.shape[-1]
    alpha = float((2 * depth) ** 0.25)
    d_main = 2 * d_a + 6 * D_ATT
    widths = (2 * d_a,) + (D_ATT,) * 6 + (LANES,)
    n_band_p = min(BAND, s)
    p_len = cache_fox_k.shape[2]

    w_in_t = jnp.transpose(w_in, (2, 0, 1))
    w_in_b = w_in_t[:d_main].reshape(d_main, depth * d).astype(BF16)
    w_f_b = jnp.pad(w_in_t[d_main:].reshape(N_HEADS, depth * d),
                    ((0, LANES - N_HEADS), (0, 0))).astype(BF16)
    w_ra_b, w_rx_b = w_rg_a.astype(BF16), w_rg_x.astype(BF16)
    w_out_b, w_up_b, w_down_b = w_out.astype(BF16), w_up.astype(BF16), w_down.astype(BF16)

    xp = x_prompt.reshape(bp * s, d)
    xs = x_sample.reshape(bs * ts, d)
    outs = [[] for _ in range(6)]
    p_kv = [jnp.zeros((depth, bp, rows_ * N_HEADS, HEAD_DIM), F32) for rows_ in (s, s, n_band_p, n_band_p)]
    s_kv = [jnp.zeros((depth, bs, ts * N_HEADS, HEAD_DIM), F32) for _ in range(4)]
    band_kc, band_vc = _interleaved(cache_band_k), _interleaved(cache_band_v)
    fox_kc, fox_vc = _interleaved(cache_fox_k), _interleaved(cache_fox_v)

    for l in range(depth):
        bf_pad = jnp.pad(b_forget[l], (0, LANES - N_HEADS)).reshape(1, LANES)
        lru_w = (w_conv[l], _row(b_conv[l]), w_ra_b[l], _row(b_rg_a[l]), w_rx_b[l], _row(b_rg_x[l]),
                 _row(lru_lambda[l]))
        rel_row = _rel_row(rel_bias[l])

        def finish(x2d, ya, yb, yc):
            x1 = _mix(x2d, ya, yb, yc, _row(g_mix[l]), w_out_b, l, _row(ln1_g[l]), _row(ln1_b[l]), alpha)
            return _ffn(x1, w_up_b, w_down_b, l, _row(ln2_g[l]), _row(ln2_b[l]), alpha)

        a2, bq, bk, bv, lf, qa, ka, vt, *p_kv = _proj_prompt(xp, w_in_b, w_f_b, l, bf_pad, widths, p_kv, s)
        ya, buf, h_last = _rglru(a2.reshape(bp, s, 2 * d_a), jnp.zeros((bp, SUBLANES, d_a), F32),
                                 jnp.zeros((bp, 1, d_a), F32), *lru_w, reset_first=True)
        r3 = lambda z: z.reshape(bp, s, D_ATT)
        yb = _band_prompt(r3(bq), r3(bk), r3(bv), rel_row)
        yc = _fox_prompt(qa, ka, vt, bp)
        xp = finish(xp, ya.reshape(bp * s, d_a), yb.reshape(bp * s, D_ATT), yc.reshape(bp * s, D_ATT))
        for dst, val in zip(outs[:3], (buf[:, SUBLANES - (CONV_W - 1):], h_last.reshape(bp, d_a),
                                       lf[:, :N_HEADS].reshape(bp, s, N_HEADS))):
            dst.append(val)

        a2, bq, bk, bv, lf, cq, ck, cv, *s_kv = _proj_sample(xs, w_in_b, w_f_b, l, bf_pad, widths, s_kv, ts)
        conv8 = jnp.pad(state_conv[l], ((0, 0), (SUBLANES - (CONV_W - 1), 0), (0, 0)))
        ya, buf, h_last = _rglru(a2.reshape(bs, ts, 2 * d_a), conv8, state_lru[l].reshape(bs, 1, d_a),
                                 *lru_w, reset_first=False)
        r3 = lambda z: z.reshape(bs, ts, D_ATT)
        yb = _band_sample(r3(bq), r3(bk), r3(bv), band_kc, band_vc, l, rel_row)
        logf = lf[:, :N_HEADS].reshape(bs, ts, N_HEADS)
        fk_cache = _cum_logf(jnp.swapaxes(cache_fox_logf[l], 1, 2), jnp.zeros((bs, N_HEADS, 1), F32))
        fk_new = _cum_logf(jnp.swapaxes(logf, 1, 2), fk_cache[:, :, p_len - 1:])
        yc = _fox_sample(r3(cq), r3(ck), r3(cv), fox_kc, fox_vc, l,
                         jnp.swapaxes(fk_new, 1, 2), fk_cache, fk_new)
        xs = finish(xs, ya.reshape(bs * ts, d_a), yb.reshape(bs * ts, D_ATT), yc.reshape(bs * ts, D_ATT))
        for dst, val in zip(outs[3:], (buf[:, SUBLANES - (CONV_W - 1):], h_last.reshape(bs, d_a), logf)):
            dst.append(val)

    st = [jnp.stack(o, axis=0) for o in outs]
    heads = lambda z: z.reshape(z.shape[:2] + (z.shape[2] // N_HEADS, N_HEADS, HEAD_DIM))
    fox_k, fox_v, band_k, band_v = [heads(z) for z in p_kv]
    return (xp.reshape(bp, s, d), xs.reshape(bs, ts, d),
            st[0], st[1], band_k, band_v, fox_k, fox_v, st[2],
            st[3], st[4], *[heads(z) for z in s_kv], st[5])
```
